```python
import math
import jax, jax.numpy as jnp
from jax import lax
import numpy as np

D_MODEL = 1024
BATCH = 16
SEQ = 2048
DEPTH = 4

CHUNK = 64
N_MIXERS = 2
RET_HEADS = 4
RET_QK_DIM = 256
RET_V_DIM = 512
RET_IN = 2 * RET_HEADS * RET_QK_DIM + 2 * RET_HEADS * RET_V_DIM
SB_HEADS = 16
SB_HEAD_DIM = D_MODEL // SB_HEADS
SB_QBLOCK = 128
FFN_HIDDEN = -(-(8 * D_MODEL) // (3 * 256)) * 256
ROPE_THETA = 10000.0
EPS = 1e-6
N_RET = (DEPTH + 1) // 2
N_SB = DEPTH // 2

kernel_name = "hybrid_retention_stickbreaking_trunk"


def rms_norm(x, gain):
    xf = x.astype(jnp.float32)
    y = xf * lax.rsqrt(jnp.mean(xf * xf, axis=-1, keepdims=True) + EPS)
    return (y * gain.astype(jnp.float32)).astype(x.dtype)


def rotary(x, positions):
    d = x.shape[-1]
    inv_freq = 1.0 / (ROPE_THETA ** (jnp.arange(0, d // 2, dtype=jnp.float32) / (d // 2)))
    ang = positions.astype(jnp.float32)[:, None] * inv_freq[None, :]
    cos = jnp.cos(ang)[None, :, None, :].astype(x.dtype)
    sin = jnp.sin(ang)[None, :, None, :].astype(x.dtype)
    x1, x2 = x[..., : d // 2], x[..., d // 2:]
    return jnp.concatenate([x1 * cos - x2 * sin, x1 * sin + x2 * cos], axis=-1)


def retention(h, w_in, out_gain, w_o):
    B, S, _ = h.shape
    H, DK, DV, C = RET_HEADS, RET_QK_DIM, RET_V_DIM, CHUNK
    N = S // C
    proj = h @ w_in
    q, k, v, g = jnp.split(proj, [H * DK, 2 * H * DK, 2 * H * DK + H * DV], axis=-1)
    pos = jnp.arange(S)
    q = rotary(q.reshape(B, S, H, DK), pos)
    k = rotary(k.reshape(B, S, H, DK), pos) * (DK ** -0.5)
    v = v.reshape(B, S, H, DV)
    to_chunks = lambda t: t.reshape(B, N, C, H, t.shape[-1]).transpose(1, 0, 3, 2, 4)
    qc, kc, vc = to_chunks(q), to_chunks(k), to_chunks(v)

    log_gamma = jnp.log(1.0 - 2.0 ** (-5.0 - jnp.arange(H, dtype=jnp.float32)))
    idx = jnp.arange(C, dtype=jnp.float32)
    intra = jnp.exp(log_gamma[:, None, None] * jnp.abs(idx[:, None] - idx[None, :])).astype(q.dtype)
    q_decay = jnp.exp(log_gamma[:, None] * (idx + 1.0)).astype(q.dtype)
    k_decay = jnp.exp(log_gamma[:, None] * (C - 1.0 - idx)).astype(q.dtype)
    chunk_decay = jnp.exp(log_gamma * C).astype(q.dtype)

    scores = jnp.einsum('nbhcd,nbhed->nbhce', qc, kc) * intra
    inner = jnp.einsum('nbhce,nbhev->nbhcv', scores, vc)

    def step(state, xs):
        qn, kn, vn = xs
        cross = jnp.einsum('bhcd,bhdv->bhcv', qn * q_decay[None, :, :, None], state)
        new_state = state * chunk_decay[None, :, None, None] + jnp.einsum(
            'bhcd,bhcv->bhdv', kn * k_decay[None, :, :, None], vn)
        return new_state, cross

    state0 = jnp.zeros((B, H, DK, DV), dtype=q.dtype)
    _, cross = lax.scan(step, state0, (qc, kc, vc))
    o = (inner + cross).transpose(1, 0, 3, 2, 4).reshape(B, S, H, DV)
    o = rms_norm(o, out_gain)
    o = o.reshape(B, S, H * DV) * jax.nn.silu(g)
    return o @ w_o


def stick_breaking(h, w_in, q_gain, k_gain, w_o):
    B, S, _ = h.shape
    H, DH, QB = SB_HEADS, SB_HEAD_DIM, SB_QBLOCK
    q, k, v = jnp.split(h @ w_in, 3, axis=-1)
    heads = lambda t: t.reshape(B, S, H, DH).transpose(0, 2, 1, 3)
    q = rms_norm(heads(q), q_gain)
    k = rms_norm(heads(k), k_gain)
    v = heads(v)
    scale = DH ** -0.5
    outs = []
    for blk in range(S // QB):
        t0 = blk * QB
        kend = t0 + QB
        z = jnp.einsum('bhtd,bhsd->bhts', q[:, :, t0:kend], k[:, :, :kend]).astype(jnp.float32) * scale
        t_pos = t0 + jnp.arange(QB)
        s_pos = jnp.arange(kend)
        mask = s_pos[None, :] < t_pos[:, None]
        log_stay = jnp.where(mask, jax.nn.log_sigmoid(-z), 0.0)
        later = lax.cumsum(log_stay, axis=3, reverse=True) - log_stay
        weight = jnp.where(mask, jnp.exp(jax.nn.log_sigmoid(z) + later), 0.0)
        outs.append(jnp.einsum('bhts,bhsd->bhtd', weight.astype(v.dtype), v[:, :, :kend]))
    o = jnp.concatenate(outs, axis=2).transpose(0, 2, 1, 3).reshape(B, S, H * DH)
    return o @ w_o


def swiglu(h, w_in, w_out):
    gate, up = jnp.split(h @ w_in, 2, axis=-1)
    return (jax.nn.silu(gate) * up) @ w_out


def setup_inputs(seed: int = 0) -> dict:
    key = jax.random.key(seed)
    ks = jax.random.split(key, 12)
    nrm = lambda k, shape, fan_in: jax.random.normal(k, shape, jnp.float32) * (fan_in ** -0.5)
    gain = lambda k, shape: 1.0 + 0.02 * jax.random.normal(k, shape, jnp.float32)
    return {
        "x": jax.random.normal(ks[0], (BATCH, SEQ, D_MODEL), jnp.float32),
        "mix_norm": gain(ks[1], (DEPTH, D_MODEL)),
        "ffn_norm": gain(ks[2], (DEPTH, D_MODEL)),
        "ret_w_in": nrm(ks[3], (N_RET, D_MODEL, RET_IN), D_MODEL),
        "ret_out_norm": gain(ks[4], (N_RET, RET_HEADS, RET_V_DIM)),
        "ret_w_o": nrm(ks[5], (N_RET, RET_HEADS * RET_V_DIM, D_MODEL), RET_HEADS * RET_V_DIM),
        "sb_w_in": nrm(ks[6], (N_SB, D_MODEL, 3 * D_MODEL), D_MODEL),
        "sb_q_norm": gain(ks[7], (N_SB, SB_HEAD_DIM)),
        "sb_k_norm": gain(ks[8], (N_SB, SB_HEAD_DIM)),
        "sb_w_o": nrm(ks[9], (N_SB, D_MODEL, D_MODEL), D_MODEL),
        "ffn_w_in": nrm(ks[10], (DEPTH, D_MODEL, 2 * FFN_HIDDEN), D_MODEL),
        "ffn_w_out": nrm(ks[11], (DEPTH, FFN_HIDDEN, D_MODEL), FFN_HIDDEN),
    }


def reference(x, mix_norm, ffn_norm, ret_w_in, ret_out_norm, ret_w_o,
              sb_w_in, sb_q_norm, sb_k_norm, sb_w_o, ffn_w_in, ffn_w_out):
    for i in range(DEPTH):
        h = rms_norm(x, mix_norm[i])
        j = i // N_MIXERS
        if i % N_MIXERS == 0:
            x = x + retention(h, ret_w_in[j], ret_out_norm[j], ret_w_o[j])
        else:
            x = x + stick_breaking(h, sb_w_in[j], sb_q_norm[j], sb_k_norm[j], sb_w_o[j])
        x = x + swiglu(rms_norm(x, ffn_norm[i]), ffn_w_in[i], ffn_w_out[i])
    return x
```

```python
import functools
import math

import jax
import jax.numpy as jnp
from jax import lax
from jax.experimental import pallas as pl
from jax.experimental.pallas import tpu as pltpu

D_MODEL = 1024
DEPTH = 4
N_MIXERS = 2
CHUNK = 64
RET_HEADS = 4
RET_QK_DIM = 256
RET_V_DIM = 512
SB_HEADS = 16
SB_HEAD_DIM = 64
FFN_HIDDEN = 2816
ROPE_THETA = 10000.0
EPS = 1e-6

LANES = 128
MXU_DIM = 256
VMEM_LIMIT = 56 * 1024 * 1024

RET_BLOCK = MXU_DIM
SB_BLOCK = MXU_DIM
ROW_TILE = 1024
FFN_ROW_TILE = 512
FFN_COL = MXU_DIM

F32 = jnp.float32
BF16 = jnp.bfloat16
NT_DIMS = (((1,), (1,)), ((), ()))
TN_DIMS = (((0,), (0,)), ((), ()))


def _rms_scale(x):
    return lax.rsqrt(jnp.mean(x * x, axis=-1, keepdims=True) + EPS)


def _norm_matmul_kernel(x_ref, g_ref, w_ref, o_ref, h_ref):
    @pl.when(pl.program_id(1) == 0)
    def _():
        x = x_ref[...]
        h_ref[...] = (x * _rms_scale(x) * g_ref[...]).astype(BF16)

    o_ref[...] = jnp.dot(h_ref[...], w_ref[...],
                         preferred_element_type=F32).astype(o_ref.dtype)


def _norm_matmul(x, gain, w, col_tile):
    t, d = x.shape
    n = w.shape[1]
    return pl.pallas_call(
        _norm_matmul_kernel,
        grid=(t // ROW_TILE, n // col_tile),
        in_specs=[
            pl.BlockSpec((ROW_TILE, d), lambda i, j: (i, 0)),
            pl.BlockSpec((1, d), lambda i, j: (0, 0)),
            pl.BlockSpec((d, col_tile), lambda i, j: (0, j)),
        ],
        out_specs=pl.BlockSpec((ROW_TILE, col_tile), lambda i, j: (i, j)),
        out_shape=jax.ShapeDtypeStruct((t, n), BF16),
        scratch_shapes=[pltpu.VMEM((ROW_TILE, d), BF16)],
        compiler_params=pltpu.CompilerParams(
            dimension_semantics=("parallel", "arbitrary"),
            vmem_limit_bytes=VMEM_LIMIT),
        name="norm_matmul",
    )(x, gain.reshape(1, d), w)


def _matmul_residual_kernel(a_ref, w_ref, x_ref, o_ref):
    o_ref[...] = x_ref[...] + jnp.dot(a_ref[...], w_ref[...],
                                      preferred_element_type=F32)


def _matmul_residual(a, w, x):
    t, k = a.shape
    d = w.shape[1]
    return pl.pallas_call(
        _matmul_residual_kernel,
        grid=(t // ROW_TILE,),
        in_specs=[
            pl.BlockSpec((ROW_TILE, k), lambda i: (i, 0)),
            pl.BlockSpec((k, d), lambda i: (0, 0)),
            pl.BlockSpec((ROW_TILE, d), lambda i: (i, 0)),
        ],
        out_specs=pl.BlockSpec((ROW_TILE, d), lambda i: (i, 0)),
        out_shape=jax.ShapeDtypeStruct((t, d), F32),
        compiler_params=pltpu.CompilerParams(
            dimension_semantics=("parallel",),
            vmem_limit_bytes=VMEM_LIMIT),
        name="matmul_residual",
    )(a, w, x)


def _ffn_kernel(x_ref, g_ref, win_ref, wout_ref, o_ref, h_ref):
    x = x_ref[...]
    h_ref[...] = (x * _rms_scale(x) * g_ref[...]).astype(BF16)
    for c in range(FFN_HIDDEN // FFN_COL):
        lo = c * FFN_COL
        h = h_ref[...]
        gate = jnp.dot(h, win_ref[:, lo:lo + FFN_COL], preferred_element_type=F32)
        up = jnp.dot(h, win_ref[:, FFN_HIDDEN + lo:FFN_HIDDEN + lo + FFN_COL],
                     preferred_element_type=F32)
        act = (gate * jax.nn.sigmoid(gate) * up).astype(BF16)
        part = jnp.dot(act, wout_ref[lo:lo + FFN_COL, :], preferred_element_type=F32)
        if c == 0:
            o_ref[...] = x_ref[...] + part
        else:
            o_ref[...] += part


def _ffn(x, gain, w_in, w_out):
    t, d = x.shape
    resident = pl.Buffered(1)
    return pl.pallas_call(
        _ffn_kernel,
        grid=(t // FFN_ROW_TILE,),
        in_specs=[
            pl.BlockSpec((FFN_ROW_TILE, d), lambda i: (i, 0)),
            pl.BlockSpec((1, d), lambda i: (0, 0)),
            pl.BlockSpec((d, 2 * FFN_HIDDEN), lambda i: (0, 0), pipeline_mode=resident),
            pl.BlockSpec((FFN_HIDDEN, d), lambda i: (0, 0), pipeline_mode=resident),
        ],
        out_specs=pl.BlockSpec((FFN_ROW_TILE, d), lambda i: (i, 0)),
        out_shape=jax.ShapeDtypeStruct((t, d), F32),
        scratch_shapes=[pltpu.VMEM((FFN_ROW_TILE, d), BF16)],
        compiler_params=pltpu.CompilerParams(
            dimension_semantics=("parallel",),
            vmem_limit_bytes=VMEM_LIMIT),
        name="ffn",
    )(x, gain.reshape(1, d), w_in, w_out)


def _retention_tables(seq):
    h = jnp.arange(RET_HEADS, dtype=F32)
    log_gamma = jnp.log(1.0 - 2.0 ** (-5.0 - h))[:, None, None]
    p = jnp.arange(RET_BLOCK)
    pf = p.astype(F32)
    diff = pf[:, None] - pf[None, :]
    same = (p[:, None] // CHUNK) == (p[None, :] // CHUNK)
    earlier = (p[:, None] // CHUNK) > (p[None, :] // CHUNK)
    mask = jnp.where(same[None], jnp.exp(log_gamma * jnp.abs(diff)[None]),
                     jnp.where(earlier[None], jnp.exp(log_gamma * diff[None]), 0.0))
    q_decay = jnp.exp(log_gamma * (pf + 1.0)[None, :, None])
    k_decay = jnp.exp(log_gamma * (RET_BLOCK - 1.0 - pf)[None, :, None])
    q_decay = jnp.broadcast_to(q_decay, (RET_HEADS, RET_BLOCK, RET_QK_DIM))
    k_decay = jnp.broadcast_to(k_decay, (RET_HEADS, RET_BLOCK, RET_QK_DIM))
    block_decay = jnp.exp(log_gamma[:, 0, 0] * RET_BLOCK)

    half = RET_QK_DIM // 2
    inv_freq = 1.0 / (ROPE_THETA ** (jnp.arange(0, half, dtype=F32) / half))
    ang = jnp.arange(seq).astype(F32)[:, None] * inv_freq[None, :]
    return mask, q_decay, k_decay, block_decay, jnp.cos(ang), jnp.sin(ang)


def _retention_kernel(bd_ref, q_ref, k_ref, v_ref, g_ref, cos_ref, sin_ref,
                      mask_ref, qd_ref, kd_ref, gain_ref, o_ref, state_ref):
    head = pl.program_id(1)

    @pl.when(pl.program_id(2) == 0)
    def _():
        state_ref[...] = jnp.zeros_like(state_ref)

    cos = cos_ref[...]
    sin = sin_ref[...]
    half = RET_QK_DIM // 2

    def rotary(t):
        t1, t2 = t[:, :half], t[:, half:]
        return jnp.concatenate([t1 * cos - t2 * sin, t1 * sin + t2 * cos], axis=-1)

    q = rotary(q_ref[...].astype(F32))
    k = rotary(k_ref[...].astype(F32)) * (RET_QK_DIM ** -0.5)
    v = v_ref[...]
    state = state_ref[...]

    scores = lax.dot_general(q.astype(BF16), k.astype(BF16), NT_DIMS,
                             preferred_element_type=F32) * mask_ref[...]
    inner = jnp.dot(scores.astype(BF16), v, preferred_element_type=F32)
    cross = jnp.dot((q * qd_ref[...]).astype(BF16), state.astype(BF16),
                    preferred_element_type=F32)
    update = lax.dot_general((k * kd_ref[...]).astype(BF16), v, TN_DIMS,
                             preferred_element_type=F32)
    state_ref[...] = state * bd_ref[head] + update

    o = inner + cross
    o = o * _rms_scale(o) * gain_ref[...]
    g = g_ref[...].astype(F32)
    o_ref[...] = (o * (g * jax.nn.sigmoid(g))).astype(o_ref.dtype)


def _retention(proj, out_gain, batch, seq):
    t = proj.shape[0]
    nblk = seq // RET_BLOCK
    mask, q_decay, k_decay, block_decay, cos, sin = _retention_tables(seq)
    qk_cols = RET_HEADS * RET_QK_DIM // RET_QK_DIM
    v_col0 = 2 * RET_HEADS * RET_QK_DIM // RET_V_DIM
    g_col0 = v_col0 + RET_HEADS
    row = lambda b, h, n: b * nblk + n
    per_head = lambda b, h, n: (h, 0, 0)
    return pl.pallas_call(
        _retention_kernel,
        grid=(batch, RET_HEADS, nblk),
        in_specs=[
            pl.BlockSpec(memory_space=pltpu.SMEM),
            pl.BlockSpec((RET_BLOCK, RET_QK_DIM), lambda b, h, n: (row(b, h, n), h)),
            pl.BlockSpec((RET_BLOCK, RET_QK_DIM), lambda b, h, n: (row(b, h, n), qk_cols + h)),
            pl.BlockSpec((RET_BLOCK, RET_V_DIM), lambda b, h, n: (row(b, h, n), v_col0 + h)),
            pl.BlockSpec((RET_BLOCK, RET_V_DIM), lambda b, h, n: (row(b, h, n), g_col0 + h)),
            pl.BlockSpec((RET_BLOCK, RET_QK_DIM // 2), lambda b, h, n: (n, 0)),
            pl.BlockSpec((RET_BLOCK, RET_QK_DIM // 2), lambda b, h, n: (n, 0)),
            pl.BlockSpec((None, RET_BLOCK, RET_BLOCK), per_head),
            pl.BlockSpec((None, RET_BLOCK, RET_QK_DIM), per_head),
            pl.BlockSpec((None, RET_BLOCK, RET_QK_DIM), per_head),
            pl.BlockSpec((None, 1, RET_V_DIM), per_head),
        ],
        out_specs=pl.BlockSpec((RET_BLOCK, RET_V_DIM), lambda b, h, n: (row(b, h, n), h)),
        out_shape=jax.ShapeDtypeStruct((t, RET_HEADS * RET_V_DIM), BF16),
        scratch_shapes=[pltpu.VMEM((RET_QK_DIM, RET_V_DIM), F32)],
        compiler_params=pltpu.CompilerParams(
            dimension_semantics=("parallel", "parallel", "arbitrary"),
            vmem_limit_bytes=VMEM_LIMIT),
        name="retention",
    )(block_decay, proj, proj, proj, proj, cos, sin, mask, q_decay, k_decay,
      out_gain.reshape(RET_HEADS, 1, RET_V_DIM))


def _sb_kernel(q_ref, k_ref, v_ref, qg_ref, kg_ref, tri_ref, o_ref,
               qs_ref, ks_ref, acc_ref, carry_ref):
    seq = q_ref.shape[0]
    nblk = seq // SB_BLOCK
    lane = lax.broadcasted_iota(jnp.int32, (seq, LANES), 1)
    first = lane < SB_HEAD_DIM

    def head_norm(x, gain):
        x2 = x * x
        s0 = jnp.sum(jnp.where(first, x2, 0.0), axis=-1, keepdims=True)
        s1 = jnp.sum(jnp.where(first, 0.0, x2), axis=-1, keepdims=True)
        ms = jnp.where(first, s0, s1) * (1.0 / SB_HEAD_DIM)
        return x * lax.rsqrt(ms + EPS) * gain

    qn = head_norm(q_ref[...].astype(F32), qg_ref[...]) * (SB_HEAD_DIM ** -0.5)
    qs_ref[0] = jnp.where(first, qn, 0.0).astype(BF16)
    qs_ref[1] = jnp.where(first, 0.0, qn).astype(BF16)
    ks_ref[...] = head_norm(k_ref[...].astype(F32), kg_ref[...]).astype(BF16)

    tri = tri_ref[...]
    row = lax.broadcasted_iota(jnp.int32, (SB_BLOCK, SB_BLOCK), 0)
    col = lax.broadcasted_iota(jnp.int32, (SB_BLOCK, SB_BLOCK), 1)
    causal = col < row
    out_first = lax.broadcasted_iota(jnp.int32, (SB_BLOCK, LANES), 1) < SB_HEAD_DIM

    def tile(h, q_tile, k0, diagonal):
        z = lax.dot_general(q_tile, ks_ref[pl.ds(k0, SB_BLOCK), :], NT_DIMS,
                            preferred_element_type=F32)
        softplus = jnp.maximum(z, 0.0) + jnp.log(1.0 + jnp.exp(-jnp.abs(z)))
        log_beta = z - softplus
        stay = jnp.where(causal, softplus, 0.0) if diagonal else softplus
        hi = stay.astype(BF16)
        lo = (stay - hi.astype(F32)).astype(BF16)
        later = (jnp.dot(hi, tri, preferred_element_type=F32)
                 + jnp.dot(lo, tri, preferred_element_type=F32))
        if diagonal:
            w = jnp.where(causal, jnp.exp(log_beta - later), 0.0)
        else:
            carry = carry_ref[...]
            w = jnp.exp(log_beta - later - jnp.concatenate([carry, carry], axis=-1))
        pv = jnp.dot(w.astype(BF16), v_ref[pl.ds(k0, SB_BLOCK), :],
                     preferred_element_type=F32)
        total = jnp.sum(stay, axis=-1, keepdims=True)
        if diagonal:
            acc_ref[h] = pv
            carry_ref[...] = jnp.broadcast_to(total, carry_ref.shape)
        else:
            acc_ref[h] += pv
            carry_ref[...] = carry + total

    def q_block(qb, _):
        q0 = pl.multiple_of(qb * SB_BLOCK, SB_BLOCK)
        for h in range(2):
            q_tile = qs_ref[h, pl.ds(q0, SB_BLOCK), :]
            tile(h, q_tile, q0, True)

            def k_block(i, _):
                k0 = pl.multiple_of((qb - 1 - i) * SB_BLOCK, SB_BLOCK)
                tile(h, q_tile, k0, False)
                return 0

            lax.fori_loop(0, qb, k_block, 0)
        o_ref[pl.ds(q0, SB_BLOCK), :] = jnp.where(
            out_first, acc_ref[0], acc_ref[1]).astype(o_ref.dtype)
        return 0

    lax.fori_loop(0, nblk, q_block, 0)


def _stick_breaking(proj, q_gain, k_gain, batch, seq):
    t = proj.shape[0]
    pairs = SB_HEADS * SB_HEAD_DIM // LANES
    j = jnp.arange(SB_BLOCK)
    tri = (j[:, None] > j[None, :]).astype(BF16)
    qg = jnp.tile(q_gain, LANES // SB_HEAD_DIM).reshape(1, LANES)
    kg = jnp.tile(k_gain, LANES // SB_HEAD_DIM).reshape(1, LANES)
    return pl.pallas_call(
        _sb_kernel,
        grid=(batch, pairs),
        in_specs=[
            pl.BlockSpec((seq, LANES), lambda b, p: (b, p)),
            pl.BlockSpec((seq, LANES), lambda b, p: (b, pairs + p)),
            pl.BlockSpec((seq, LANES), lambda b, p: (b, 2 * pairs + p)),
            pl.BlockSpec((1, LANES), lambda b, p: (0, 0)),
            pl.BlockSpec((1, LANES), lambda b, p: (0, 0)),
            pl.BlockSpec((SB_BLOCK, SB_BLOCK), lambda b, p: (0, 0)),
        ],
        out_specs=pl.BlockSpec((seq, LANES), lambda b, p: (b, p)),
        out_shape=jax.ShapeDtypeStruct((t, SB_HEADS * SB_HEAD_DIM), BF16),
        scratch_shapes=[
            pltpu.VMEM((2, seq, LANES), BF16),
            pltpu.VMEM((seq, LANES), BF16),
            pltpu.VMEM((2, SB_BLOCK, LANES), F32),
            pltpu.VMEM((SB_BLOCK, LANES), F32),
        ],
        compiler_params=pltpu.CompilerParams(
            dimension_semantics=("parallel", "parallel"),
            vmem_limit_bytes=VMEM_LIMIT),
        name="stick_breaking",
    )(proj, proj, proj, qg, kg, tri)


def kernel(x, mix_norm, ffn_norm, ret_w_in, ret_out_norm, ret_w_o, sb_w_in,
           sb_q_norm, sb_k_norm, sb_w_o, ffn_w_in, ffn_w_out):
    batch, seq, d = x.shape
    assert d == D_MODEL and seq % RET_BLOCK == 0 and seq % SB_BLOCK == 0
    assert (batch * seq) % ROW_TILE == 0
    xt = x.reshape(batch * seq, d)
    for i in range(DEPTH):
        j = i // N_MIXERS
        if i % N_MIXERS == 0:
            proj = _norm_matmul(xt, mix_norm[i], ret_w_in[j].astype(BF16), ROW_TILE)
            mixed = _retention(proj, ret_out_norm[j], batch, seq)
            xt = _matmul_residual(mixed, ret_w_o[j].astype(BF16), xt)
        else:
            proj = _norm_matmul(xt, mix_norm[i], sb_w_in[j].astype(BF16), ROW_TILE)
            mixed = _stick_breaking(proj, sb_q_norm[j], sb_k_norm[j], batch, seq)
            xt = _matmul_residual(mixed, sb_w_o[j].astype(BF16), xt)
        xt = _ffn(xt, ffn_norm[i], ffn_w_in[i].astype(BF16), ffn_w_out[i].astype(BF16))
    return xt.reshape(batch, seq, d)
```

```python
import math

import jax
import jax.numpy as jnp
from jax import lax
from jax.experimental import pallas as pl
from jax.experimental.pallas import tpu as pltpu

D_MODEL = 1024
DEPTH = 4
N_MIXERS = 2
CHUNK = 64
RET_HEADS = 4
RET_QK_DIM = 256
RET_V_DIM = 512
SB_HEADS = 16
SB_HEAD_DIM = 64
FFN_HIDDEN = 2816
ROPE_THETA = 10000.0
EPS = 1e-6

LANES = 128
MXU_DIM = 256
VMEM_LIMIT = 56 * 1024 * 1024

RET_BLOCK = MXU_DIM
SB_BLOCK = MXU_DIM
SB_GROUP = 8
ROW_TILE = 1024
FFN_ROW_TILE = 512
FFN_COL = MXU_DIM

F32 = jnp.float32
BF16 = jnp.bfloat16
NT_DIMS = (((1,), (1,)), ((), ()))
TN_DIMS = (((0,), (0,)), ((), ()))
LOG2E = math.log2(math.e)


def _rms_scale(x):
    return lax.rsqrt(jnp.mean(x * x, axis=-1, keepdims=True) + EPS)


def _split_bf16(x):
    hi = x.astype(BF16)
    return hi, (x - hi.astype(F32)).astype(BF16)


def _norm_matmul_kernel(x_ref, g_ref, w_ref, o_ref, h_ref):
    @pl.when(pl.program_id(1) == 0)
    def _():
        x = x_ref[...]
        h_ref[...] = (x * _rms_scale(x) * g_ref[...]).astype(BF16)

    o_ref[...] = jnp.dot(h_ref[...], w_ref[...],
                         preferred_element_type=F32).astype(o_ref.dtype)


def _norm_matmul(x, gain, w, col_tile):
    t, d = x.shape
    n = w.shape[1]
    return pl.pallas_call(
        _norm_matmul_kernel,
        grid=(t // ROW_TILE, n // col_tile),
        in_specs=[
            pl.BlockSpec((ROW_TILE, d), lambda i, j: (i, 0)),
            pl.BlockSpec((1, d), lambda i, j: (0, 0)),
            pl.BlockSpec((d, col_tile), lambda i, j: (0, j)),
        ],
        out_specs=pl.BlockSpec((ROW_TILE, col_tile), lambda i, j: (i, j)),
        out_shape=jax.ShapeDtypeStruct((t, n), BF16),
        scratch_shapes=[pltpu.VMEM((ROW_TILE, d), BF16)],
        compiler_params=pltpu.CompilerParams(
            dimension_semantics=("parallel", "arbitrary"),
            vmem_limit_bytes=VMEM_LIMIT),
        name="norm_matmul",
    )(x, gain.reshape(1, d), w)


def _matmul_residual_kernel(a_ref, w_ref, x_ref, o_ref):
    o_ref[...] = x_ref[...] + jnp.dot(a_ref[...], w_ref[...],
                                      preferred_element_type=F32)


def _matmul_residual(a, w, x):
    t, k = a.shape
    d = w.shape[1]
    return pl.pallas_call(
        _matmul_residual_kernel,
        grid=(t // ROW_TILE,),
        in_specs=[
            pl.BlockSpec((ROW_TILE, k), lambda i: (i, 0)),
            pl.BlockSpec((k, d), lambda i: (0, 0)),
            pl.BlockSpec((ROW_TILE, d), lambda i: (i, 0)),
        ],
        out_specs=pl.BlockSpec((ROW_TILE, d), lambda i: (i, 0)),
        out_shape=jax.ShapeDtypeStruct((t, d), F32),
        compiler_params=pltpu.CompilerParams(
            dimension_semantics=("parallel",),
            vmem_limit_bytes=VMEM_LIMIT),
        name="matmul_residual",
    )(a, w, x)


def _ffn_kernel(x_ref, g_ref, win_ref, wout_ref, o_ref, h_ref):
    x = x_ref[...]
    h_ref[...] = (x * _rms_scale(x) * g_ref[...]).astype(BF16)
    for c in range(FFN_HIDDEN // FFN_COL):
        lo = c * FFN_COL
        h = h_ref[...]
        gate = jnp.dot(h, win_ref[:, lo:lo + FFN_COL], preferred_element_type=F32)
        up = jnp.dot(h, win_ref[:, FFN_HIDDEN + lo:FFN_HIDDEN + lo + FFN_COL],
                     preferred_element_type=F32)
        act = (gate * jax.nn.sigmoid(gate) * up).astype(BF16)
        part = jnp.dot(act, wout_ref[lo:lo + FFN_COL, :], preferred_element_type=F32)
        if c == 0:
            o_ref[...] = x_ref[...] + part
        else:
            o_ref[...] += part


def _ffn(x, gain, w_in, w_out):
    t, d = x.shape
    resident = pl.Buffered(1)
    return pl.pallas_call(
        _ffn_kernel,
        grid=(t // FFN_ROW_TILE,),
        in_specs=[
            pl.BlockSpec((FFN_ROW_TILE, d), lambda i: (i, 0)),
            pl.BlockSpec((1, d), lambda i: (0, 0)),
            pl.BlockSpec((d, 2 * FFN_HIDDEN), lambda i: (0, 0), pipeline_mode=resident),
            pl.BlockSpec((FFN_HIDDEN, d), lambda i: (0, 0), pipeline_mode=resident),
        ],
        out_specs=pl.BlockSpec((FFN_ROW_TILE, d), lambda i: (i, 0)),
        out_shape=jax.ShapeDtypeStruct((t, d), F32),
        scratch_shapes=[pltpu.VMEM((FFN_ROW_TILE, d), BF16)],
        compiler_params=pltpu.CompilerParams(
            dimension_semantics=("parallel",),
            vmem_limit_bytes=VMEM_LIMIT),
        name="ffn",
    )(x, gain.reshape(1, d), w_in, w_out)


def _retention_tables(seq):
    h = jnp.arange(RET_HEADS, dtype=F32)
    log_gamma = jnp.log(1.0 - 2.0 ** (-5.0 - h))[:, None, None]
    p = jnp.arange(RET_BLOCK)
    pf = p.astype(F32)
    diff = pf[:, None] - pf[None, :]
    same = (p[:, None] // CHUNK) == (p[None, :] // CHUNK)
    earlier = (p[:, None] // CHUNK) > (p[None, :] // CHUNK)
    mask = jnp.where(same[None], jnp.exp(log_gamma * jnp.abs(diff)[None]),
                     jnp.where(earlier[None], jnp.exp(log_gamma * diff[None]), 0.0))
    q_decay = jnp.exp(log_gamma * (pf + 1.0)[None, :, None])
    k_decay = jnp.exp(log_gamma * (RET_BLOCK - 1.0 - pf)[None, :, None])
    q_decay = jnp.broadcast_to(q_decay, (RET_HEADS, RET_BLOCK, RET_QK_DIM))
    k_decay = jnp.broadcast_to(k_decay, (RET_HEADS, RET_BLOCK, RET_QK_DIM))
    block_decay = jnp.exp(log_gamma[:, 0, 0] * RET_BLOCK)

    half = RET_QK_DIM // 2
    inv_freq = 1.0 / (ROPE_THETA ** (jnp.arange(0, half, dtype=F32) / half))
    ang = jnp.arange(seq).astype(F32)[:, None] * inv_freq[None, :]
    return mask, q_decay, k_decay, block_decay, jnp.cos(ang), jnp.sin(ang)


def _retention_kernel(bd_ref, p_ref, cos_ref, sin_ref, mask_ref, qd_ref, kd_ref,
                      gain_ref, o_ref, state_ref):
    @pl.when(pl.program_id(1) == 0)
    def _():
        state_ref[...] = jnp.zeros_like(state_ref)

    cos = cos_ref[...]
    sin = sin_ref[...]
    half = RET_QK_DIM // 2
    k_col0 = RET_HEADS * RET_QK_DIM
    v_col0 = 2 * k_col0
    g_col0 = v_col0 + RET_HEADS * RET_V_DIM

    def rotary(t):
        t1, t2 = t[:, :half], t[:, half:]
        return jnp.concatenate([t1 * cos - t2 * sin, t1 * sin + t2 * cos], axis=-1)

    for h in range(RET_HEADS):
        qk = slice(h * RET_QK_DIM, (h + 1) * RET_QK_DIM)
        vg = slice(h * RET_V_DIM, (h + 1) * RET_V_DIM)
        q = rotary(p_ref[:, qk].astype(F32))
        k = rotary(p_ref[:, k_col0 + qk.start:k_col0 + qk.stop].astype(F32)) * (RET_QK_DIM ** -0.5)
        v = p_ref[:, v_col0 + vg.start:v_col0 + vg.stop]
        state = state_ref[h]

        scores = lax.dot_general(q.astype(BF16), k.astype(BF16), NT_DIMS,
                                 preferred_element_type=F32) * mask_ref[h]
        inner = jnp.dot(scores.astype(BF16), v, preferred_element_type=F32)
        cross = jnp.dot((q * qd_ref[h]).astype(BF16), state.astype(BF16),
                        preferred_element_type=F32)
        update = lax.dot_general((k * kd_ref[h]).astype(BF16), v, TN_DIMS,
                                 preferred_element_type=F32)
        state_ref[h] = state * bd_ref[h] + update

        o = inner + cross
        o = o * _rms_scale(o) * gain_ref[h]
        g = p_ref[:, g_col0 + vg.start:g_col0 + vg.stop].astype(F32)
        o_ref[:, vg] = (o * (g * jax.nn.sigmoid(g))).astype(o_ref.dtype)


def _retention(proj, out_gain, batch, seq):
    t, width = proj.shape
    nblk = seq // RET_BLOCK
    mask, q_decay, k_decay, block_decay, cos, sin = _retention_tables(seq)
    whole = lambda b, n: (0, 0, 0)
    return pl.pallas_call(
        _retention_kernel,
        grid=(batch, nblk),
        in_specs=[
            pl.BlockSpec(memory_space=pltpu.SMEM),
            pl.BlockSpec((RET_BLOCK, width), lambda b, n: (b * nblk + n, 0)),
            pl.BlockSpec((RET_BLOCK, RET_QK_DIM // 2), lambda b, n: (n, 0)),
            pl.BlockSpec((RET_BLOCK, RET_QK_DIM // 2), lambda b, n: (n, 0)),
            pl.BlockSpec((RET_HEADS, RET_BLOCK, RET_BLOCK), whole),
            pl.BlockSpec((RET_HEADS, RET_BLOCK, RET_QK_DIM), whole),
            pl.BlockSpec((RET_HEADS, RET_BLOCK, RET_QK_DIM), whole),
            pl.BlockSpec((RET_HEADS, 1, RET_V_DIM), whole),
        ],
        out_specs=pl.BlockSpec((RET_BLOCK, RET_HEADS * RET_V_DIM),
                               lambda b, n: (b * nblk + n, 0)),
        out_shape=jax.ShapeDtypeStruct((t, RET_HEADS * RET_V_DIM), BF16),
        scratch_shapes=[pltpu.VMEM((RET_HEADS, RET_QK_DIM, RET_V_DIM), F32)],
        compiler_params=pltpu.CompilerParams(
            dimension_semantics=("parallel", "arbitrary"),
            vmem_limit_bytes=VMEM_LIMIT),
        name="retention",
    )(block_decay, proj, cos, sin, mask, q_decay, k_decay,
      out_gain.reshape(RET_HEADS, 1, RET_V_DIM))


def _sb_kernel(q_ref, k_ref, v_ref, qg_ref, kg_ref, o_ref,
               qs_ref, ks_ref, vt_ref, acc_ref, carry_ref):
    seq = q_ref.shape[0]
    nblk = seq // SB_BLOCK
    npair = q_ref.shape[1] // LANES
    per_block = LANES // SB_HEAD_DIM
    heads = range(per_block * npair)

    first = lax.broadcasted_iota(jnp.int32, (seq, LANES), 1) < SB_HEAD_DIM
    seg_r = lax.broadcasted_iota(jnp.int32, (LANES, LANES), 0) // SB_HEAD_DIM
    seg_c = lax.broadcasted_iota(jnp.int32, (LANES, LANES), 1) // SB_HEAD_DIM
    same_head = jnp.where(seg_r == seg_c, 1.0, 0.0).astype(BF16)
    same_head2 = jnp.concatenate([same_head, same_head], axis=0)

    def head_norm(x, gain):
        ss = jnp.dot(jnp.concatenate(_split_bf16(x * x), axis=-1), same_head2,
                     preferred_element_type=F32)
        return x * lax.rsqrt(ss * (1.0 / SB_HEAD_DIM) + EPS) * gain

    for p in range(npair):
        cols = slice(p * LANES, (p + 1) * LANES)
        qn = head_norm(q_ref[:, cols].astype(F32), qg_ref[...]) * (LOG2E * SB_HEAD_DIM ** -0.5)
        qs_ref[per_block * p] = jnp.where(first, qn, 0.0).astype(BF16)
        qs_ref[per_block * p + 1] = jnp.where(first, 0.0, qn).astype(BF16)
        ks_ref[p] = head_norm(k_ref[:, cols].astype(F32), kg_ref[...]).astype(BF16)
        for kb in range(nblk):
            rows = slice(kb * SB_BLOCK, (kb + 1) * SB_BLOCK)
            vt_ref[p, kb] = v_ref[rows, cols].astype(F32).T.astype(BF16)

    row = lax.broadcasted_iota(jnp.int32, (SB_BLOCK, SB_BLOCK), 0)
    col = lax.broadcasted_iota(jnp.int32, (SB_BLOCK, SB_BLOCK), 1)
    causal = row < col
    tri = jnp.where(row <= col, 1.0, 0.0).astype(BF16)
    tri2 = jnp.concatenate([tri, tri], axis=1)
    sign_bit = jnp.uint32(0x80000000)

    def tiles(q0, kb, diagonal):
        k0 = pl.multiple_of(kb * SB_BLOCK, SB_BLOCK)
        zs = [lax.dot_general(ks_ref[h // per_block, pl.ds(k0, SB_BLOCK), :],
                              qs_ref[h, pl.ds(q0, SB_BLOCK), :], NT_DIMS,
                              preferred_element_type=F32) for h in heads]
        from_here = []
        for h in heads:
            z = zs[h]
            neg_abs = lax.bitcast_convert_type(
                lax.bitcast_convert_type(z, jnp.uint32) | sign_bit, F32)
            softplus = jnp.maximum(z, 0.0) + jnp.log(1.0 + jnp.exp2(neg_abs)) * LOG2E
            stay = jnp.where(causal, softplus, 0.0) if diagonal else softplus
            from_here.append(jnp.dot(tri2, jnp.concatenate(_split_bf16(stay), axis=0),
                                     preferred_element_type=F32))
        for h in heads:
            p, sub = divmod(h, per_block)
            total = from_here[h][0:1, :]
            if diagonal:
                w = jnp.where(causal, jnp.exp2(zs[h] - from_here[h]), 0.0)
                carry_ref[h] = total
            else:
                carry = carry_ref[h]
                w = jnp.exp2(zs[h] - from_here[h] - carry)
                carry_ref[h] = carry + total
            v_t = vt_ref[p, kb, sub * SB_HEAD_DIM:(sub + 1) * SB_HEAD_DIM, :]
            pv = jnp.dot(v_t, w.astype(BF16), preferred_element_type=F32)
            out_rows = slice(h * SB_HEAD_DIM, (h + 1) * SB_HEAD_DIM)
            if diagonal:
                acc_ref[out_rows, :] = pv
            else:
                acc_ref[out_rows, :] += pv

    def q_block(qb, _):
        q0 = pl.multiple_of(qb * SB_BLOCK, SB_BLOCK)
        tiles(q0, qb, True)

        def k_block(i, _):
            tiles(q0, qb - 1 - i, False)
            return 0

        lax.fori_loop(0, qb, k_block, 0)
        o_ref[pl.ds(q0, SB_BLOCK), :] = acc_ref[...].T.astype(o_ref.dtype)
        return 0

    lax.fori_loop(0, nblk, q_block, 0)


def _stick_breaking(proj, q_gain, k_gain, batch, seq):
    t = proj.shape[0]
    width = SB_GROUP * SB_HEAD_DIM
    groups = SB_HEADS // SB_GROUP
    qg = jnp.tile(q_gain, LANES // SB_HEAD_DIM).reshape(1, LANES)
    kg = jnp.tile(k_gain, LANES // SB_HEAD_DIM).reshape(1, LANES)
    return pl.pallas_call(
        _sb_kernel,
        grid=(batch, groups),
        in_specs=[
            pl.BlockSpec((seq, width), lambda b, g: (b, g)),
            pl.BlockSpec((seq, width), lambda b, g: (b, groups + g)),
            pl.BlockSpec((seq, width), lambda b, g: (b, 2 * groups + g)),
            pl.BlockSpec((1, LANES), lambda b, g: (0, 0)),
            pl.BlockSpec((1, LANES), lambda b, g: (0, 0)),
        ],
        out_specs=pl.BlockSpec((seq, width), lambda b, g: (b, g)),
        out_shape=jax.ShapeDtypeStruct((t, SB_HEADS * SB_HEAD_DIM), BF16),
        scratch_shapes=[
            pltpu.VMEM((SB_GROUP, seq, LANES), BF16),
            pltpu.VMEM((width // LANES, seq, LANES), BF16),
            pltpu.VMEM((width // LANES, seq // SB_BLOCK, LANES, SB_BLOCK), BF16),
            pltpu.VMEM((width, SB_BLOCK), F32),
            pltpu.VMEM((SB_GROUP, 1, SB_BLOCK), F32),
        ],
        compiler_params=pltpu.CompilerParams(
            dimension_semantics=("parallel", "parallel"),
            vmem_limit_bytes=VMEM_LIMIT),
        name="stick_breaking",
    )(proj, proj, proj, qg, kg)


def kernel(x, mix_norm, ffn_norm, ret_w_in, ret_out_norm, ret_w_o, sb_w_in,
           sb_q_norm, sb_k_norm, sb_w_o, ffn_w_in, ffn_w_out):
    batch, seq, d = x.shape
    assert d == D_MODEL and seq % RET_BLOCK == 0 and seq % SB_BLOCK == 0
    assert (batch * seq) % ROW_TILE == 0
    xt = x.reshape(batch * seq, d)
    for i in range(DEPTH):
        j = i // N_MIXERS
        if i % N_MIXERS == 0:
            proj = _norm_matmul(xt, mix_norm[i], ret_w_in[j].astype(BF16), ROW_TILE)
            mixed = _retention(proj, ret_out_norm[j], batch, seq)
            xt = _matmul_residual(mixed, ret_w_o[j].astype(BF16), xt)
        else:
            proj = _norm_matmul(xt, mix_norm[i], sb_w_in[j].astype(BF16), ROW_TILE)
            mixed = _stick_breaking(proj, sb_q_norm[j], sb_k_norm[j], batch, seq)
            xt = _matmul_residual(mixed, sb_w_o[j].astype(BF16), xt)
        xt = _ffn(xt, ffn_norm[i], ffn_w_in[i].astype(BF16), ffn_w_out[i].astype(BF16))
    return xt.reshape(batch, seq, d)
```

```python
import math

import jax
import jax.numpy as jnp
from jax import lax
from jax.experimental import pallas as pl
from jax.experimental.pallas import tpu as pltpu

D_MODEL = 1024
DEPTH = 4
N_MIXERS = 2
CHUNK = 64
RET_HEADS = 4
RET_QK_DIM = 256
RET_V_DIM = 512
SB_HEADS = 16
SB_HEAD_DIM = 64
FFN_HIDDEN = 2816
ROPE_THETA = 10000.0
EPS = 1e-6

LANES = 128
MXU_DIM = 256
VMEM_LIMIT = 56 * 1024 * 1024

RET_BLOCK = MXU_DIM
SB_BLOCK = MXU_DIM
SB_GROUP = 8
SB_DEAD_LOG2 = 160.0
PROJ_ROW_TILE = 512
PROJ_COL = 512
FFN_ROW_TILE = 512
FFN_COL = MXU_DIM

F32 = jnp.float32
BF16 = jnp.bfloat16
NT_DIMS = (((1,), (1,)), ((), ()))
TN_DIMS = (((0,), (0,)), ((), ()))
LOG2E = math.log2(math.e)


def _rms_scale(x):
    return lax.rsqrt(jnp.mean(x * x, axis=-1, keepdims=True) + EPS)


def _split_bf16(x):
    hi = x.astype(BF16)
    return hi, (x - hi.astype(F32)).astype(BF16)


def _norm_matmul_kernel(x_ref, g_ref, w_ref, o_ref, h_ref):
    x = x_ref[...]
    h_ref[...] = (x * _rms_scale(x) * g_ref[...]).astype(BF16)
    for c in range(w_ref.shape[1] // PROJ_COL):
        cols = slice(c * PROJ_COL, (c + 1) * PROJ_COL)
        o_ref[:, cols] = jnp.dot(h_ref[...], w_ref[:, cols],
                                 preferred_element_type=F32).astype(o_ref.dtype)


def _norm_matmul(x, gain, w):
    t, d = x.shape
    n = w.shape[1]
    return pl.pallas_call(
        _norm_matmul_kernel,
        grid=(t // PROJ_ROW_TILE,),
        in_specs=[
            pl.BlockSpec((PROJ_ROW_TILE, d), lambda i: (i, 0)),
            pl.BlockSpec((1, d), lambda i: (0, 0)),
            pl.BlockSpec((d, n), lambda i: (0, 0), pipeline_mode=pl.Buffered(1)),
        ],
        out_specs=pl.BlockSpec((PROJ_ROW_TILE, n), lambda i: (i, 0)),
        out_shape=jax.ShapeDtypeStruct((t, n), BF16),
        scratch_shapes=[pltpu.VMEM((PROJ_ROW_TILE, d), BF16)],
        compiler_params=pltpu.CompilerParams(
            dimension_semantics=("parallel",),
            vmem_limit_bytes=VMEM_LIMIT),
        name="norm_matmul",
    )(x, gain.reshape(1, d), w)


def _out_ffn_kernel(a_ref, wo_ref, x_ref, g_ref, win_ref, wout_ref, o_ref, h_ref):
    x = x_ref[...] + jnp.dot(a_ref[...], wo_ref[...], preferred_element_type=F32)
    o_ref[...] = x
    h_ref[...] = (x * _rms_scale(x) * g_ref[...]).astype(BF16)
    for c in range(FFN_HIDDEN // FFN_COL):
        lo = c * FFN_COL
        h = h_ref[...]
        gate = jnp.dot(h, win_ref[:, lo:lo + FFN_COL], preferred_element_type=F32)
        up = jnp.dot(h, win_ref[:, FFN_HIDDEN + lo:FFN_HIDDEN + lo + FFN_COL],
                     preferred_element_type=F32)
        act = (gate * jax.nn.sigmoid(gate) * up).astype(BF16)
        o_ref[...] += jnp.dot(act, wout_ref[lo:lo + FFN_COL, :], preferred_element_type=F32)


def _out_ffn(a, w_o, x, gain, w_in, w_out):
    t, d = x.shape
    k = a.shape[1]
    resident = pl.Buffered(1)
    rows = lambda i: (i, 0)
    whole = lambda i: (0, 0)
    return pl.pallas_call(
        _out_ffn_kernel,
        grid=(t // FFN_ROW_TILE,),
        in_specs=[
            pl.BlockSpec((FFN_ROW_TILE, k), rows),
            pl.BlockSpec((k, d), whole, pipeline_mode=resident),
            pl.BlockSpec((FFN_ROW_TILE, d), rows),
            pl.BlockSpec((1, d), whole),
            pl.BlockSpec((d, 2 * FFN_HIDDEN), whole, pipeline_mode=resident),
            pl.BlockSpec((FFN_HIDDEN, d), whole, pipeline_mode=resident),
        ],
        out_specs=pl.BlockSpec((FFN_ROW_TILE, d), rows),
        out_shape=jax.ShapeDtypeStruct((t, d), F32),
        scratch_shapes=[pltpu.VMEM((FFN_ROW_TILE, d), BF16)],
        compiler_params=pltpu.CompilerParams(
            dimension_semantics=("parallel",),
            vmem_limit_bytes=VMEM_LIMIT),
        name="out_ffn",
    )(a, w_o, x, gain.reshape(1, d), w_in, w_out)


def _retention_tables(seq):
    h = jnp.arange(RET_HEADS, dtype=F32)
    log_gamma = jnp.log(1.0 - 2.0 ** (-5.0 - h))[:, None, None]
    p = jnp.arange(RET_BLOCK)
    pf = p.astype(F32)
    diff = pf[:, None] - pf[None, :]
    same = (p[:, None] // CHUNK) == (p[None, :] // CHUNK)
    earlier = (p[:, None] // CHUNK) > (p[None, :] // CHUNK)
    mask = jnp.where(same[None], jnp.exp(log_gamma * jnp.abs(diff)[None]),
                     jnp.where(earlier[None], jnp.exp(log_gamma * diff[None]), 0.0))
    q_decay = jnp.exp(log_gamma * (pf + 1.0)[None, :, None])
    k_decay = jnp.exp(log_gamma * (RET_BLOCK - 1.0 - pf)[None, :, None])
    q_decay = jnp.broadcast_to(q_decay, (RET_HEADS, RET_BLOCK, RET_QK_DIM))
    k_decay = jnp.broadcast_to(k_decay, (RET_HEADS, RET_BLOCK, RET_QK_DIM))
    block_decay = jnp.exp(log_gamma[:, 0, 0] * RET_BLOCK)

    half = RET_QK_DIM // 2
    inv_freq = 1.0 / (ROPE_THETA ** (jnp.arange(0, half, dtype=F32) / half))
    ang = jnp.arange(seq).astype(F32)[:, None] * inv_freq[None, :]
    return mask, q_decay, k_decay, block_decay, jnp.cos(ang), jnp.sin(ang)


def _retention_kernel(bd_ref, p_ref, cos_ref, sin_ref, mask_ref, qd_ref, kd_ref,
                      gain_ref, o_ref, state_ref):
    @pl.when(pl.program_id(1) == 0)
    def _():
        state_ref[...] = jnp.zeros_like(state_ref)

    cos = cos_ref[...]
    sin = sin_ref[...]
    half = RET_QK_DIM // 2
    k_col0 = RET_HEADS * RET_QK_DIM
    v_col0 = 2 * k_col0
    g_col0 = v_col0 + RET_HEADS * RET_V_DIM

    def rotary(t):
        t1, t2 = t[:, :half], t[:, half:]
        return jnp.concatenate([t1 * cos - t2 * sin, t1 * sin + t2 * cos], axis=-1)

    for h in range(RET_HEADS):
        qk = slice(h * RET_QK_DIM, (h + 1) * RET_QK_DIM)
        vg = slice(h * RET_V_DIM, (h + 1) * RET_V_DIM)
        q = rotary(p_ref[:, qk].astype(F32))
        k = rotary(p_ref[:, k_col0 + qk.start:k_col0 + qk.stop].astype(F32)) * (RET_QK_DIM ** -0.5)
        v = p_ref[:, v_col0 + vg.start:v_col0 + vg.stop]
        state = state_ref[h]

        scores = lax.dot_general(q.astype(BF16), k.astype(BF16), NT_DIMS,
                                 preferred_element_type=F32) * mask_ref[h]
        inner = jnp.dot(scores.astype(BF16), v, preferred_element_type=F32)
        cross = jnp.dot((q * qd_ref[h]).astype(BF16), state.astype(BF16),
                        preferred_element_type=F32)
        update = lax.dot_general((k * kd_ref[h]).astype(BF16), v, TN_DIMS,
                                 preferred_element_type=F32)
        state_ref[h] = state * bd_ref[h] + update

        o = inner + cross
        o = o * _rms_scale(o) * gain_ref[h]
        g = p_ref[:, g_col0 + vg.start:g_col0 + vg.stop].astype(F32)
        o_ref[:, vg] = (o * (g * jax.nn.sigmoid(g))).astype(o_ref.dtype)


def _retention(proj, out_gain, batch, seq):
    t, width = proj.shape
    nblk = seq // RET_BLOCK
    mask, q_decay, k_decay, block_decay, cos, sin = _retention_tables(seq)
    whole = lambda b, n: (0, 0, 0)
    return pl.pallas_call(
        _retention_kernel,
        grid=(batch, nblk),
        in_specs=[
            pl.BlockSpec(memory_space=pltpu.SMEM),
            pl.BlockSpec((RET_BLOCK, width), lambda b, n: (b * nblk + n, 0)),
            pl.BlockSpec((RET_BLOCK, RET_QK_DIM // 2), lambda b, n: (n, 0)),
            pl.BlockSpec((RET_BLOCK, RET_QK_DIM // 2), lambda b, n: (n, 0)),
            pl.BlockSpec((RET_HEADS, RET_BLOCK, RET_BLOCK), whole),
            pl.BlockSpec((RET_HEADS, RET_BLOCK, RET_QK_DIM), whole),
            pl.BlockSpec((RET_HEADS, RET_BLOCK, RET_QK_DIM), whole),
            pl.BlockSpec((RET_HEADS, 1, RET_V_DIM), whole),
        ],
        out_specs=pl.BlockSpec((RET_BLOCK, RET_HEADS * RET_V_DIM),
                               lambda b, n: (b * nblk + n, 0)),
        out_shape=jax.ShapeDtypeStruct((t, RET_HEADS * RET_V_DIM), BF16),
        scratch_shapes=[pltpu.VMEM((RET_HEADS, RET_QK_DIM, RET_V_DIM), F32)],
        compiler_params=pltpu.CompilerParams(
            dimension_semantics=("parallel", "arbitrary"),
            vmem_limit_bytes=VMEM_LIMIT),
        name="retention",
    )(block_decay, proj, cos, sin, mask, q_decay, k_decay,
      out_gain.reshape(RET_HEADS, 1, RET_V_DIM))


def _sb_kernel(q_ref, k_ref, v_ref, qg_ref, kg_ref, o_ref,
               qs_ref, ks_ref, vt_ref, acc_ref, carry_ref):
    seq = q_ref.shape[0]
    nblk = seq // SB_BLOCK
    npair = q_ref.shape[1] // LANES
    per_block = LANES // SB_HEAD_DIM
    heads = range(per_block * npair)

    first = lax.broadcasted_iota(jnp.int32, (seq, LANES), 1) < SB_HEAD_DIM
    seg_r = lax.broadcasted_iota(jnp.int32, (LANES, LANES), 0) // SB_HEAD_DIM
    seg_c = lax.broadcasted_iota(jnp.int32, (LANES, LANES), 1) // SB_HEAD_DIM
    same_head = jnp.where(seg_r == seg_c, 1.0, 0.0).astype(BF16)
    same_head2 = jnp.concatenate([same_head, same_head], axis=0)

    def head_norm(x, gain):
        ss = jnp.dot(jnp.concatenate(_split_bf16(x * x), axis=-1), same_head2,
                     preferred_element_type=F32)
        return x * lax.rsqrt(ss * (1.0 / SB_HEAD_DIM) + EPS) * gain

    for p in range(npair):
        cols = slice(p * LANES, (p + 1) * LANES)
        qn = head_norm(q_ref[:, cols].astype(F32), qg_ref[...]) * (LOG2E * SB_HEAD_DIM ** -0.5)
        qs_ref[per_block * p] = jnp.where(first, qn, 0.0).astype(BF16)
        qs_ref[per_block * p + 1] = jnp.where(first, 0.0, qn).astype(BF16)
        ks_ref[p] = head_norm(k_ref[:, cols].astype(F32), kg_ref[...]).astype(BF16)
        for kb in range(nblk):
            rows = slice(kb * SB_BLOCK, (kb + 1) * SB_BLOCK)
            vt_ref[p, kb] = v_ref[rows, cols].astype(F32).T.astype(BF16)

    row = lax.broadcasted_iota(jnp.int32, (SB_BLOCK, SB_BLOCK), 0)
    col = lax.broadcasted_iota(jnp.int32, (SB_BLOCK, SB_BLOCK), 1)
    causal = row < col
    tri = jnp.where(row <= col, 1.0, 0.0).astype(BF16)
    tri2 = jnp.concatenate([tri, tri], axis=1)

    def tiles(q0, kb, diagonal):
        k0 = pl.multiple_of(kb * SB_BLOCK, SB_BLOCK)
        zs = [lax.dot_general(ks_ref[h // per_block, pl.ds(k0, SB_BLOCK), :],
                              qs_ref[h, pl.ds(q0, SB_BLOCK), :], NT_DIMS,
                              preferred_element_type=F32) for h in heads]
        from_here = []
        for h in heads:
            z = zs[h]
            softplus = jnp.maximum(z, 0.0) + jnp.log(1.0 + jnp.exp2(-jnp.abs(z))) * LOG2E
            stay = jnp.where(causal, softplus, 0.0) if diagonal else softplus
            from_here.append(jnp.dot(tri2, jnp.concatenate(_split_bf16(stay), axis=0),
                                     preferred_element_type=F32))
        least = None
        for h in heads:
            p, sub = divmod(h, per_block)
            total = from_here[h][0:1, :]
            if diagonal:
                w = jnp.where(causal, jnp.exp2(zs[h] - from_here[h]), 0.0)
                carry_ref[h] = total
            else:
                carry = carry_ref[h]
                w = jnp.exp2(zs[h] - from_here[h] - carry)
                total = carry + total
                carry_ref[h] = total
                least = total if least is None else jnp.minimum(least, total)
            v_t = vt_ref[p, kb, sub * SB_HEAD_DIM:(sub + 1) * SB_HEAD_DIM, :]
            pv = jnp.dot(v_t, w.astype(BF16), preferred_element_type=F32)
            out_rows = slice(h * SB_HEAD_DIM, (h + 1) * SB_HEAD_DIM)
            if diagonal:
                acc_ref[out_rows, :] = pv
            else:
                acc_ref[out_rows, :] += pv
        return None if diagonal else jnp.min(least)

    def q_block(qb, _):
        q0 = pl.multiple_of(qb * SB_BLOCK, SB_BLOCK)
        tiles(q0, qb, True)

        def more(state):
            i, least = state
            return jnp.logical_and(i < qb, least < SB_DEAD_LOG2)

        def k_block(state):
            i, _ = state
            return i + 1, tiles(q0, qb - 1 - i, False)

        lax.while_loop(more, k_block, (jnp.int32(0), jnp.float32(0.0)))
        o_ref[pl.ds(q0, SB_BLOCK), :] = acc_ref[...].T.astype(o_ref.dtype)
        return 0

    lax.fori_loop(0, nblk, q_block, 0)


def _stick_breaking(proj, q_gain, k_gain, batch, seq):
    t = proj.shape[0]
    width = SB_GROUP * SB_HEAD_DIM
    groups = SB_HEADS // SB_GROUP
    qg = jnp.tile(q_gain, LANES // SB_HEAD_DIM).reshape(1, LANES)
    kg = jnp.tile(k_gain, LANES // SB_HEAD_DIM).reshape(1, LANES)
    return pl.pallas_call(
        _sb_kernel,
        grid=(batch, groups),
        in_specs=[
            pl.BlockSpec((seq, width), lambda b, g: (b, g)),
            pl.BlockSpec((seq, width), lambda b, g: (b, groups + g)),
            pl.BlockSpec((seq, width), lambda b, g: (b, 2 * groups + g)),
            pl.BlockSpec((1, LANES), lambda b, g: (0, 0)),
            pl.BlockSpec((1, LANES), lambda b, g: (0, 0)),
        ],
        out_specs=pl.BlockSpec((seq, width), lambda b, g: (b, g)),
        out_shape=jax.ShapeDtypeStruct((t, SB_HEADS * SB_HEAD_DIM), BF16),
        scratch_shapes=[
            pltpu.VMEM((SB_GROUP, seq, LANES), BF16),
            pltpu.VMEM((width // LANES, seq, LANES), BF16),
            pltpu.VMEM((width // LANES, seq // SB_BLOCK, LANES, SB_BLOCK), BF16),
            pltpu.VMEM((width, SB_BLOCK), F32),
            pltpu.VMEM((SB_GROUP, 1, SB_BLOCK), F32),
        ],
        compiler_params=pltpu.CompilerParams(
            dimension_semantics=("parallel", "parallel"),
            vmem_limit_bytes=VMEM_LIMIT),
        name="stick_breaking",
    )(proj, proj, proj, qg, kg)


def kernel(x, mix_norm, ffn_norm, ret_w_in, ret_out_norm, ret_w_o, sb_w_in,
           sb_q_norm, sb_k_norm, sb_w_o, ffn_w_in, ffn_w_out):
    batch, seq, d = x.shape
    assert d == D_MODEL and seq % RET_BLOCK == 0 and seq % SB_BLOCK == 0
    assert (batch * seq) % PROJ_ROW_TILE == 0 and (batch * seq) % FFN_ROW_TILE == 0
    xt = x.reshape(batch * seq, d)
    for i in range(DEPTH):
        j = i // N_MIXERS
        if i % N_MIXERS == 0:
            proj = _norm_matmul(xt, mix_norm[i], ret_w_in[j].astype(BF16))
            mixed = _retention(proj, ret_out_norm[j], batch, seq)
            w_o = ret_w_o[j]
        else:
            proj = _norm_matmul(xt, mix_norm[i], sb_w_in[j].astype(BF16))
            mixed = _stick_breaking(proj, sb_q_norm[j], sb_k_norm[j], batch, seq)
            w_o = sb_w_o[j]
        xt = _out_ffn(mixed, w_o.astype(BF16), xt, ffn_norm[i],
                      ffn_w_in[i].astype(BF16), ffn_w_out[i].astype(BF16))
    return xt.reshape(batch, seq, d)
```

```python
import functools
import math

import jax
import jax.numpy as jnp
from jax import lax
from jax.experimental import pallas as pl
from jax.experimental.pallas import tpu as pltpu

D_MODEL = 1024
DEPTH = 4
N_MIXERS = 2
CHUNK = 64
RET_HEADS = 4
RET_QK_DIM = 256
RET_V_DIM = 512
SB_HEADS = 16
SB_HEAD_DIM = 64
FFN_HIDDEN = 2816
ROPE_THETA = 10000.0
EPS = 1e-6

LANES = 128
MXU_DIM = 256
VMEM_LIMIT = 56 * 1024 * 1024

RET_BLOCK = MXU_DIM
SB_BLOCK = MXU_DIM
SB_GROUP = 8
SB_DEAD_LOG2 = 160.0
PROJ_ROW_TILE = 512
PROJ_COL = 512
FFN_ROW_TILE = 512
FFN_COL = MXU_DIM

F32 = jnp.float32
BF16 = jnp.bfloat16
NT_DIMS = (((1,), (1,)), ((), ()))
TN_DIMS = (((0,), (0,)), ((), ()))
LOG2E = math.log2(math.e)


def _rms_scale(x):
    return lax.rsqrt(jnp.mean(x * x, axis=-1, keepdims=True) + EPS)


def _split_bf16(x):
    hi = x.astype(BF16)
    return hi, (x - hi.astype(F32)).astype(BF16)


def _head_rms_norm(y, gain):
    first = lax.broadcasted_iota(jnp.int32, y.shape, 1) < SB_HEAD_DIM
    y2 = y * y
    s0 = jnp.sum(jnp.where(first, y2, 0.0), axis=-1, keepdims=True)
    s1 = jnp.sum(jnp.where(first, 0.0, y2), axis=-1, keepdims=True)
    ms = jnp.where(first, s0, s1) * (1.0 / SB_HEAD_DIM)
    return y * lax.rsqrt(ms + EPS) * gain


def _norm_matmul_kernel(x_ref, g_ref, w_ref, hg_ref, o_ref, h_ref, *, head_norm_cols):
    x = x_ref[...]
    h_ref[...] = (x * _rms_scale(x) * g_ref[...]).astype(BF16)
    for c in range(w_ref.shape[1] // PROJ_COL):
        lo = c * PROJ_COL
        y = jnp.dot(h_ref[...], w_ref[:, lo:lo + PROJ_COL], preferred_element_type=F32)
        if lo < head_norm_cols:
            for b in range(PROJ_COL // LANES):
                cols = slice(lo + b * LANES, lo + (b + 1) * LANES)
                o_ref[:, cols] = _head_rms_norm(
                    y[:, b * LANES:(b + 1) * LANES], hg_ref[:, cols]).astype(o_ref.dtype)
        else:
            o_ref[:, lo:lo + PROJ_COL] = y.astype(o_ref.dtype)


def _norm_matmul(x, gain, w, head_gain=None):
    t, d = x.shape
    n = w.shape[1]
    head_norm_cols = 0 if head_gain is None else head_gain.shape[0]
    assert head_norm_cols % PROJ_COL == 0
    hg = jnp.ones((1, n), F32)
    if head_gain is not None:
        hg = hg.at[0, :head_norm_cols].set(head_gain)
    return pl.pallas_call(
        functools.partial(_norm_matmul_kernel, head_norm_cols=head_norm_cols),
        grid=(t // PROJ_ROW_TILE,),
        in_specs=[
            pl.BlockSpec((PROJ_ROW_TILE, d), lambda i: (i, 0)),
            pl.BlockSpec((1, d), lambda i: (0, 0)),
            pl.BlockSpec((d, n), lambda i: (0, 0), pipeline_mode=pl.Buffered(1)),
            pl.BlockSpec((1, n), lambda i: (0, 0)),
        ],
        out_specs=pl.BlockSpec((PROJ_ROW_TILE, n), lambda i: (i, 0)),
        out_shape=jax.ShapeDtypeStruct((t, n), BF16),
        scratch_shapes=[pltpu.VMEM((PROJ_ROW_TILE, d), BF16)],
        compiler_params=pltpu.CompilerParams(
            dimension_semantics=("parallel",),
            vmem_limit_bytes=VMEM_LIMIT),
        name="norm_matmul",
    )(x, gain.reshape(1, d), w, hg)


def _out_ffn_kernel(a_ref, wo_ref, x_ref, g_ref, win_ref, wout_ref, o_ref, h_ref):
    x = x_ref[...] + jnp.dot(a_ref[...], wo_ref[...], preferred_element_type=F32)
    o_ref[...] = x
    h_ref[...] = (x * _rms_scale(x) * g_ref[...]).astype(BF16)
    for c in range(FFN_HIDDEN // FFN_COL):
        lo = c * FFN_COL
        h = h_ref[...]
        gate = jnp.dot(h, win_ref[:, lo:lo + FFN_COL], preferred_element_type=F32)
        up = jnp.dot(h, win_ref[:, FFN_HIDDEN + lo:FFN_HIDDEN + lo + FFN_COL],
                     preferred_element_type=F32)
        act = (gate * jax.nn.sigmoid(gate) * up).astype(BF16)
        o_ref[...] += jnp.dot(act, wout_ref[lo:lo + FFN_COL, :], preferred_element_type=F32)


def _out_ffn(a, w_o, x, gain, w_in, w_out):
    t, d = x.shape
    k = a.shape[1]
    resident = pl.Buffered(1)
    rows = lambda i: (i, 0)
    whole = lambda i: (0, 0)
    return pl.pallas_call(
        _out_ffn_kernel,
        grid=(t // FFN_ROW_TILE,),
        in_specs=[
            pl.BlockSpec((FFN_ROW_TILE, k), rows),
            pl.BlockSpec((k, d), whole, pipeline_mode=resident),
            pl.BlockSpec((FFN_ROW_TILE, d), rows),
            pl.BlockSpec((1, d), whole),
            pl.BlockSpec((d, 2 * FFN_HIDDEN), whole, pipeline_mode=resident),
            pl.BlockSpec((FFN_HIDDEN, d), whole, pipeline_mode=resident),
        ],
        out_specs=pl.BlockSpec((FFN_ROW_TILE, d), rows),
        out_shape=jax.ShapeDtypeStruct((t, d), F32),
        scratch_shapes=[pltpu.VMEM((FFN_ROW_TILE, d), BF16)],
        compiler_params=pltpu.CompilerParams(
            dimension_semantics=("parallel",),
            vmem_limit_bytes=VMEM_LIMIT),
        name="out_ffn",
    )(a, w_o, x, gain.reshape(1, d), w_in, w_out)


def _retention_tables(seq):
    h = jnp.arange(RET_HEADS, dtype=F32)
    log_gamma = jnp.log(1.0 - 2.0 ** (-5.0 - h))[:, None, None]
    p = jnp.arange(RET_BLOCK)
    pf = p.astype(F32)
    diff = pf[:, None] - pf[None, :]
    same = (p[:, None] // CHUNK) == (p[None, :] // CHUNK)
    earlier = (p[:, None] // CHUNK) > (p[None, :] // CHUNK)
    k_scale = RET_QK_DIM ** -0.5
    mask = k_scale * jnp.where(same[None], jnp.exp(log_gamma * jnp.abs(diff)[None]),
                               jnp.where(earlier[None], jnp.exp(log_gamma * diff[None]), 0.0))
    q_decay = jnp.exp(log_gamma * (pf + 1.0)[None, :, None])
    k_decay = k_scale * jnp.exp(log_gamma * (RET_BLOCK - 1.0 - pf)[None, :, None])
    q_decay = jnp.broadcast_to(q_decay, (RET_HEADS, RET_BLOCK, RET_QK_DIM))
    k_decay = jnp.broadcast_to(k_decay, (RET_HEADS, RET_BLOCK, RET_QK_DIM))
    block_decay = jnp.exp(log_gamma[:, 0, 0] * RET_BLOCK)

    half = RET_QK_DIM // 2
    inv_freq = 1.0 / (ROPE_THETA ** (jnp.arange(0, half, dtype=F32) / half))
    ang = jnp.arange(seq).astype(F32)[:, None] * inv_freq[None, :]
    return mask, q_decay, k_decay, block_decay, jnp.cos(ang), jnp.sin(ang)


def _retention_kernel(bd_ref, p_ref, cos_ref, sin_ref, mask_ref, qd_ref, kd_ref,
                      gain_ref, o_ref, state_ref):
    @pl.when(pl.program_id(1) == 0)
    def _():
        state_ref[...] = jnp.zeros_like(state_ref)

    cos = cos_ref[...]
    sin = sin_ref[...]
    half = RET_QK_DIM // 2
    k_col0 = RET_HEADS * RET_QK_DIM
    v_col0 = 2 * k_col0
    g_col0 = v_col0 + RET_HEADS * RET_V_DIM

    def rotary(t):
        t1, t2 = t[:, :half], t[:, half:]
        return jnp.concatenate([t1 * cos - t2 * sin, t1 * sin + t2 * cos], axis=-1)

    for h in range(RET_HEADS):
        qk = slice(h * RET_QK_DIM, (h + 1) * RET_QK_DIM)
        vg = slice(h * RET_V_DIM, (h + 1) * RET_V_DIM)
        q = rotary(p_ref[:, qk].astype(F32))
        k = rotary(p_ref[:, k_col0 + qk.start:k_col0 + qk.stop].astype(F32))
        v = p_ref[:, v_col0 + vg.start:v_col0 + vg.stop]
        state = state_ref[h]

        scores = lax.dot_general(q.astype(BF16), k.astype(BF16), NT_DIMS,
                                 preferred_element_type=F32) * mask_ref[h]
        o = jnp.dot(
            jnp.concatenate([scores.astype(BF16), (q * qd_ref[h]).astype(BF16)], axis=1),
            jnp.concatenate([v, state.astype(BF16)], axis=0),
            preferred_element_type=F32)
        update = lax.dot_general((k * kd_ref[h]).astype(BF16), v, TN_DIMS,
                                 preferred_element_type=F32)
        state_ref[h] = state * bd_ref[h] + update

        o = o * _rms_scale(o) * gain_ref[h]
        half_g = 0.5 * p_ref[:, g_col0 + vg.start:g_col0 + vg.stop].astype(F32)
        o_ref[:, vg] = (o * (half_g + half_g * jnp.tanh(half_g))).astype(o_ref.dtype)


def _retention(proj, out_gain, batch, seq):
    t, width = proj.shape
    nblk = seq // RET_BLOCK
    mask, q_decay, k_decay, block_decay, cos, sin = _retention_tables(seq)
    whole = lambda b, n: (0, 0, 0)
    return pl.pallas_call(
        _retention_kernel,
        grid=(batch, nblk),
        in_specs=[
            pl.BlockSpec(memory_space=pltpu.SMEM),
            pl.BlockSpec((RET_BLOCK, width), lambda b, n: (b * nblk + n, 0)),
            pl.BlockSpec((RET_BLOCK, RET_QK_DIM // 2), lambda b, n: (n, 0)),
            pl.BlockSpec((RET_BLOCK, RET_QK_DIM // 2), lambda b, n: (n, 0)),
            pl.BlockSpec((RET_HEADS, RET_BLOCK, RET_BLOCK), whole),
            pl.BlockSpec((RET_HEADS, RET_BLOCK, RET_QK_DIM), whole),
            pl.BlockSpec((RET_HEADS, RET_BLOCK, RET_QK_DIM), whole),
            pl.BlockSpec((RET_HEADS, 1, RET_V_DIM), whole),
        ],
        out_specs=pl.BlockSpec((RET_BLOCK, RET_HEADS * RET_V_DIM),
                               lambda b, n: (b * nblk + n, 0)),
        out_shape=jax.ShapeDtypeStruct((t, RET_HEADS * RET_V_DIM), BF16),
        scratch_shapes=[pltpu.VMEM((RET_HEADS, RET_QK_DIM, RET_V_DIM), F32)],
        compiler_params=pltpu.CompilerParams(
            dimension_semantics=("parallel", "arbitrary"),
            vmem_limit_bytes=VMEM_LIMIT),
        name="retention",
    )(block_decay, proj, cos, sin, mask, q_decay, k_decay,
      out_gain.reshape(RET_HEADS, 1, RET_V_DIM))


def _sb_kernel(q_ref, k_ref, v_ref, o_ref, qs_ref, vt_ref, acc_ref, carry_ref):
    seq = q_ref.shape[0]
    nblk = seq // SB_BLOCK
    npair = q_ref.shape[1] // LANES
    per_block = LANES // SB_HEAD_DIM
    heads = range(per_block * npair)

    lane = lax.broadcasted_iota(jnp.int32, (1, LANES), 1)
    for p in range(npair):
        cols = slice(p * LANES, (p + 1) * LANES)
        q = q_ref[:, cols]
        for sub in range(per_block):
            keep = jnp.where(lane // SB_HEAD_DIM == sub, 1.0, 0.0).astype(BF16)
            qs_ref[per_block * p + sub] = q * keep
        for kb in range(nblk):
            rows = slice(kb * SB_BLOCK, (kb + 1) * SB_BLOCK)
            vt_ref[p, kb] = v_ref[rows, cols].astype(F32).T.astype(BF16)

    row = lax.broadcasted_iota(jnp.int32, (SB_BLOCK, SB_BLOCK), 0)
    col = lax.broadcasted_iota(jnp.int32, (SB_BLOCK, SB_BLOCK), 1)
    causal = row < col
    tri = jnp.where(row <= col, 1.0, 0.0).astype(BF16)
    tri2 = jnp.concatenate([tri, tri], axis=1)

    def tile_group(q0, kbs, from_diagonal):
        jobs = [(j, h) for j in range(len(kbs)) for h in heads]
        k0s = [pl.multiple_of(kb * SB_BLOCK, SB_BLOCK) for kb in kbs]
        zs = {}
        for j, h in jobs:
            p = h // per_block
            zs[j, h] = lax.dot_general(
                k_ref[pl.ds(k0s[j], SB_BLOCK), p * LANES:(p + 1) * LANES],
                qs_ref[h, pl.ds(q0, SB_BLOCK), :], NT_DIMS, preferred_element_type=F32)
        from_here = {}
        for j, h in jobs:
            z = zs[j, h]
            softplus = jnp.maximum(z, 0.0) + jnp.log(1.0 + jnp.exp2(-jnp.abs(z))) * LOG2E
            stay = jnp.where(causal, softplus, 0.0) if from_diagonal and j == 0 else softplus
            from_here[j, h] = jnp.dot(tri2, jnp.concatenate(_split_bf16(stay), axis=0),
                                      preferred_element_type=F32)
        carries, outs = {}, {}
        for j, h in jobs:
            p, sub = divmod(h, per_block)
            total = from_here[j, h][0:1, :]
            if from_diagonal and j == 0:
                w = jnp.where(causal, jnp.exp2(zs[j, h] - from_here[j, h]), 0.0)
                carries[h] = total
            else:
                carry = carries[h] if h in carries else carry_ref[h]
                w = jnp.exp2(zs[j, h] - from_here[j, h] - carry)
                carries[h] = carry + total
            v_t = vt_ref[p, kbs[j], sub * SB_HEAD_DIM:(sub + 1) * SB_HEAD_DIM, :]
            pv = jnp.dot(v_t, w.astype(BF16), preferred_element_type=F32)
            outs[h] = pv if h not in outs else outs[h] + pv
        least = None
        for h in heads:
            out_rows = slice(h * SB_HEAD_DIM, (h + 1) * SB_HEAD_DIM)
            if from_diagonal:
                acc_ref[out_rows, :] = outs[h]
            else:
                acc_ref[out_rows, :] += outs[h]
            carry_ref[h] = carries[h]
            least = carries[h] if least is None else jnp.minimum(least, carries[h])
        return jnp.min(least)

    def finish(q0):
        o_ref[pl.ds(q0, SB_BLOCK), :] = acc_ref[...].T.astype(o_ref.dtype)

    tile_group(0, [0], True)
    finish(0)

    def q_block(qb, _):
        q0 = pl.multiple_of(qb * SB_BLOCK, SB_BLOCK)
        least = tile_group(q0, [qb, qb - 1], True)

        def more(state):
            kb, least = state
            return jnp.logical_and(kb >= 0, least < SB_DEAD_LOG2)

        def k_block(state):
            kb, _ = state
            return kb - 1, tile_group(q0, [kb], False)

        lax.while_loop(more, k_block, (qb - 2, least))
        finish(q0)
        return 0

    lax.fori_loop(1, nblk, q_block, 0)


def _stick_breaking(proj, batch, seq):
    t = proj.shape[0]
    width = SB_GROUP * SB_HEAD_DIM
    groups = SB_HEADS // SB_GROUP
    return pl.pallas_call(
        _sb_kernel,
        grid=(batch, groups),
        in_specs=[
            pl.BlockSpec((seq, width), lambda b, g: (b, g)),
            pl.BlockSpec((seq, width), lambda b, g: (b, groups + g)),
            pl.BlockSpec((seq, width), lambda b, g: (b, 2 * groups + g)),
        ],
        out_specs=pl.BlockSpec((seq, width), lambda b, g: (b, g)),
        out_shape=jax.ShapeDtypeStruct((t, SB_HEADS * SB_HEAD_DIM), BF16),
        scratch_shapes=[
            pltpu.VMEM((SB_GROUP, seq, LANES), BF16),
            pltpu.VMEM((width // LANES, seq // SB_BLOCK, LANES, SB_BLOCK), BF16),
            pltpu.VMEM((width, SB_BLOCK), F32),
            pltpu.VMEM((SB_GROUP, 1, SB_BLOCK), F32),
        ],
        compiler_params=pltpu.CompilerParams(
            dimension_semantics=("parallel", "parallel"),
            vmem_limit_bytes=VMEM_LIMIT),
        name="stick_breaking",
    )(proj, proj, proj)


def kernel(x, mix_norm, ffn_norm, ret_w_in, ret_out_norm, ret_w_o, sb_w_in,
           sb_q_norm, sb_k_norm, sb_w_o, ffn_w_in, ffn_w_out):
    batch, seq, d = x.shape
    assert d == D_MODEL and seq % RET_BLOCK == 0 and seq % SB_BLOCK == 0
    assert (batch * seq) % PROJ_ROW_TILE == 0 and (batch * seq) % FFN_ROW_TILE == 0
    xt = x.reshape(batch * seq, d)
    for i in range(DEPTH):
        j = i // N_MIXERS
        if i % N_MIXERS == 0:
            proj = _norm_matmul(xt, mix_norm[i], ret_w_in[j].astype(BF16))
            mixed = _retention(proj, ret_out_norm[j], batch, seq)
            w_o = ret_w_o[j]
        else:
            head_gain = jnp.concatenate([
                jnp.tile(sb_q_norm[j] * (LOG2E * SB_HEAD_DIM ** -0.5), SB_HEADS),
                jnp.tile(sb_k_norm[j], SB_HEADS)])
            proj = _norm_matmul(xt, mix_norm[i], sb_w_in[j].astype(BF16), head_gain)
            mixed = _stick_breaking(proj, batch, seq)
            w_o = sb_w_o[j]
        xt = _out_ffn(mixed, w_o.astype(BF16), xt, ffn_norm[i],
                      ffn_w_in[i].astype(BF16), ffn_w_out[i].astype(BF16))
    return xt.reshape(batch, seq, d)
```

```python
import functools
import math

import jax
import jax.numpy as jnp
from jax import lax
from jax.experimental import pallas as pl
from jax.experimental.pallas import tpu as pltpu

D_MODEL = 1024
DEPTH = 4
N_MIXERS = 2
CHUNK = 64
RET_HEADS = 4
RET_QK_DIM = 256
RET_V_DIM = 512
SB_HEADS = 16
SB_HEAD_DIM = 64
FFN_HIDDEN = 2816
ROPE_THETA = 10000.0
EPS = 1e-6

LANES = 128
MXU_DIM = 256
VMEM_LIMIT = 56 * 1024 * 1024

RET_BLOCK = MXU_DIM
SB_BLOCK = MXU_DIM
SB_GROUP = 8
SB_DEAD_LOG2 = 160.0
PROJ_ROW_TILE = 512
PROJ_COL = 512
FFN_ROW_TILE = 512
FFN_COL = MXU_DIM

F32 = jnp.float32
BF16 = jnp.bfloat16
NT_DIMS = (((1,), (1,)), ((), ()))
TN_DIMS = (((0,), (0,)), ((), ()))
LOG2E = math.log2(math.e)


def _rms_scale(x):
    return lax.rsqrt(jnp.mean(x * x, axis=-1, keepdims=True) + EPS)


def _split_bf16(x):
    hi = x.astype(BF16)
    return hi, (x - hi.astype(F32)).astype(BF16)


def _head_rms_norm(y, gain):
    first = lax.broadcasted_iota(jnp.int32, y.shape, 1) < SB_HEAD_DIM
    y2 = y * y
    s0 = jnp.sum(jnp.where(first, y2, 0.0), axis=-1, keepdims=True)
    s1 = jnp.sum(jnp.where(first, 0.0, y2), axis=-1, keepdims=True)
    ms = jnp.where(first, s0, s1) * (1.0 / SB_HEAD_DIM)
    return y * lax.rsqrt(ms + EPS) * gain


def _norm_matmul_kernel(x_ref, g_ref, w_ref, hg_ref, o_ref, h_ref, *, head_norm_cols):
    x = x_ref[...]
    h_ref[...] = (x * _rms_scale(x) * g_ref[...]).astype(BF16)
    for c in range(w_ref.shape[1] // PROJ_COL):
        lo = c * PROJ_COL
        y = jnp.dot(h_ref[...], w_ref[:, lo:lo + PROJ_COL], preferred_element_type=F32)
        if lo < head_norm_cols:
            for b in range(PROJ_COL // LANES):
                cols = slice(lo + b * LANES, lo + (b + 1) * LANES)
                o_ref[:, cols] = _head_rms_norm(
                    y[:, b * LANES:(b + 1) * LANES], hg_ref[:, cols]).astype(o_ref.dtype)
        else:
            o_ref[:, lo:lo + PROJ_COL] = y.astype(o_ref.dtype)


def _norm_matmul(x, gain, w, head_gain=None):
    t, d = x.shape
    n = w.shape[1]
    head_norm_cols = 0 if head_gain is None else head_gain.shape[0]
    assert head_norm_cols % PROJ_COL == 0
    hg = jnp.ones((1, n), F32)
    if head_gain is not None:
        hg = hg.at[0, :head_norm_cols].set(head_gain)
    return pl.pallas_call(
        functools.partial(_norm_matmul_kernel, head_norm_cols=head_norm_cols),
        grid=(t // PROJ_ROW_TILE,),
        in_specs=[
            pl.BlockSpec((PROJ_ROW_TILE, d), lambda i: (i, 0)),
            pl.BlockSpec((1, d), lambda i: (0, 0)),
            pl.BlockSpec((d, n), lambda i: (0, 0), pipeline_mode=pl.Buffered(1)),
            pl.BlockSpec((1, n), lambda i: (0, 0)),
        ],
        out_specs=pl.BlockSpec((PROJ_ROW_TILE, n), lambda i: (i, 0)),
        out_shape=jax.ShapeDtypeStruct((t, n), BF16),
        scratch_shapes=[pltpu.VMEM((PROJ_ROW_TILE, d), BF16)],
        compiler_params=pltpu.CompilerParams(
            dimension_semantics=("parallel",),
            vmem_limit_bytes=VMEM_LIMIT),
        name="norm_matmul",
    )(x, gain.reshape(1, d), w, hg)


def _out_ffn_kernel(a_ref, wo_ref, x_ref, g_ref, win_ref, wout_ref, o_ref, h_ref):
    x = x_ref[...] + jnp.dot(a_ref[...], wo_ref[...], preferred_element_type=F32)
    o_ref[...] = x
    h_ref[...] = (x * _rms_scale(x) * g_ref[...]).astype(BF16)
    for c in range(FFN_HIDDEN // FFN_COL):
        lo = c * FFN_COL
        h = h_ref[...]
        gate = jnp.dot(h, win_ref[:, lo:lo + FFN_COL], preferred_element_type=F32)
        up = jnp.dot(h, win_ref[:, FFN_HIDDEN + lo:FFN_HIDDEN + lo + FFN_COL],
                     preferred_element_type=F32)
        act = (gate * jax.nn.sigmoid(gate) * up).astype(BF16)
        o_ref[...] += jnp.dot(act, wout_ref[lo:lo + FFN_COL, :], preferred_element_type=F32)


def _out_ffn(a, w_o, x, gain, w_in, w_out):
    t, d = x.shape
    k = a.shape[1]
    resident = pl.Buffered(1)
    rows = lambda i: (i, 0)
    whole = lambda i: (0, 0)
    return pl.pallas_call(
        _out_ffn_kernel,
        grid=(t // FFN_ROW_TILE,),
        in_specs=[
            pl.BlockSpec((FFN_ROW_TILE, k), rows),
            pl.BlockSpec((k, d), whole, pipeline_mode=resident),
            pl.BlockSpec((FFN_ROW_TILE, d), rows),
            pl.BlockSpec((1, d), whole),
            pl.BlockSpec((d, 2 * FFN_HIDDEN), whole, pipeline_mode=resident),
            pl.BlockSpec((FFN_HIDDEN, d), whole, pipeline_mode=resident),
        ],
        out_specs=pl.BlockSpec((FFN_ROW_TILE, d), rows),
        out_shape=jax.ShapeDtypeStruct((t, d), F32),
        scratch_shapes=[pltpu.VMEM((FFN_ROW_TILE, d), BF16)],
        compiler_params=pltpu.CompilerParams(
            dimension_semantics=("parallel",),
            vmem_limit_bytes=VMEM_LIMIT),
        name="out_ffn",
    )(a, w_o, x, gain.reshape(1, d), w_in, w_out)


RET_K_COL0 = RET_HEADS * RET_QK_DIM
RET_V_COL0 = 2 * RET_HEADS * RET_QK_DIM
RET_G_COL0 = RET_V_COL0 + RET_HEADS * RET_V_DIM
RET_QDEC_COL0 = RET_G_COL0 + RET_HEADS * RET_V_DIM
RET_KDEC_COL0 = RET_QDEC_COL0 + RET_HEADS * RET_QK_DIM
RET_PROJ_WIDTH = RET_KDEC_COL0 + RET_HEADS * RET_QK_DIM


def _retention_tables(seq):
    h = jnp.arange(RET_HEADS, dtype=F32)
    log_gamma = jnp.log(1.0 - 2.0 ** (-5.0 - h))[:, None, None]
    p = jnp.arange(RET_BLOCK)
    pf = p.astype(F32)
    diff = pf[:, None] - pf[None, :]
    same = (p[:, None] // CHUNK) == (p[None, :] // CHUNK)
    earlier = (p[:, None] // CHUNK) > (p[None, :] // CHUNK)
    k_scale = RET_QK_DIM ** -0.5
    mask = k_scale * jnp.where(same[None], jnp.exp(log_gamma * jnp.abs(diff)[None]),
                               jnp.where(earlier[None], jnp.exp(log_gamma * diff[None]), 0.0))
    tile_pos = (jnp.arange(PROJ_ROW_TILE) % RET_BLOCK).astype(F32)[None, :, None]
    q_decay = jnp.exp(log_gamma * (tile_pos + 1.0))
    k_decay = k_scale * jnp.exp(log_gamma * (RET_BLOCK - 1.0 - tile_pos))
    q_decay = jnp.broadcast_to(q_decay, (RET_HEADS, PROJ_ROW_TILE, LANES))
    k_decay = jnp.broadcast_to(k_decay, (RET_HEADS, PROJ_ROW_TILE, LANES))
    block_decay = jnp.exp(log_gamma[:, 0, 0] * RET_BLOCK)

    half = RET_QK_DIM // 2
    inv_freq = 1.0 / (ROPE_THETA ** (jnp.arange(0, half, dtype=F32) / half))
    ang = jnp.arange(seq).astype(F32)[:, None] * inv_freq[None, :]
    return mask, q_decay, k_decay, block_decay, jnp.cos(ang), jnp.sin(ang)


def _ret_proj_kernel(x_ref, g_ref, w_ref, cos_ref, sin_ref, qd_ref, kd_ref, o_ref, h_ref):
    x = x_ref[...]
    h_ref[...] = (x * _rms_scale(x) * g_ref[...]).astype(BF16)
    cos = cos_ref[...]
    sin = sin_ref[...]
    half = RET_QK_DIM // 2

    for c in range(w_ref.shape[1] // PROJ_COL):
        lo = c * PROJ_COL
        y = jnp.dot(h_ref[...], w_ref[:, lo:lo + PROJ_COL], preferred_element_type=F32)
        if lo < RET_V_COL0:
            is_q = lo < RET_K_COL0
            dec_ref = qd_ref if is_q else kd_ref
            dec_col0 = RET_QDEC_COL0 if is_q else RET_KDEC_COL0
            rel = lo - (0 if is_q else RET_K_COL0)
            for hh in range(PROJ_COL // RET_QK_DIM):
                head = rel // RET_QK_DIM + hh
                y1 = y[:, hh * RET_QK_DIM:hh * RET_QK_DIM + half]
                y2 = y[:, hh * RET_QK_DIM + half:(hh + 1) * RET_QK_DIM]
                decay = dec_ref[head]
                for part, r in enumerate((y1 * cos - y2 * sin, y1 * sin + y2 * cos)):
                    off = hh * RET_QK_DIM + part * half
                    o_ref[:, lo + off:lo + off + half] = r.astype(o_ref.dtype)
                    o_ref[:, dec_col0 + rel + off:dec_col0 + rel + off + half] = (
                        r * decay).astype(o_ref.dtype)
        elif lo < RET_G_COL0:
            o_ref[:, lo:lo + PROJ_COL] = y.astype(o_ref.dtype)
        else:
            half_g = 0.5 * y
            o_ref[:, lo:lo + PROJ_COL] = (half_g + half_g * jnp.tanh(half_g)).astype(o_ref.dtype)


def _ret_proj(x, gain, w, cos, sin, q_decay, k_decay, seq):
    t, d = x.shape
    n = w.shape[1]
    tiles_per_seq = seq // PROJ_ROW_TILE
    pos = lambda i: (i % tiles_per_seq, 0)
    whole3 = lambda i: (0, 0, 0)
    return pl.pallas_call(
        _ret_proj_kernel,
        grid=(t // PROJ_ROW_TILE,),
        in_specs=[
            pl.BlockSpec((PROJ_ROW_TILE, d), lambda i: (i, 0)),
            pl.BlockSpec((1, d), lambda i: (0, 0)),
            pl.BlockSpec((d, n), lambda i: (0, 0), pipeline_mode=pl.Buffered(1)),
            pl.BlockSpec((PROJ_ROW_TILE, RET_QK_DIM // 2), pos),
            pl.BlockSpec((PROJ_ROW_TILE, RET_QK_DIM // 2), pos),
            pl.BlockSpec((RET_HEADS, PROJ_ROW_TILE, LANES), whole3),
            pl.BlockSpec((RET_HEADS, PROJ_ROW_TILE, LANES), whole3),
        ],
        out_specs=pl.BlockSpec((PROJ_ROW_TILE, RET_PROJ_WIDTH), lambda i: (i, 0)),
        out_shape=jax.ShapeDtypeStruct((t, RET_PROJ_WIDTH), BF16),
        scratch_shapes=[pltpu.VMEM((PROJ_ROW_TILE, d), BF16)],
        compiler_params=pltpu.CompilerParams(
            dimension_semantics=("parallel",),
            vmem_limit_bytes=VMEM_LIMIT),
        name="ret_proj",
    )(x, gain.reshape(1, d), w, cos, sin, q_decay, k_decay)


def _retention_kernel(bd_ref, p_ref, mask_ref, gain_ref, o_ref, state_ref):
    @pl.when(pl.program_id(1) == 0)
    def _():
        state_ref[...] = jnp.zeros_like(state_ref)

    heads = range(RET_HEADS)
    qk = [slice(h * RET_QK_DIM, (h + 1) * RET_QK_DIM) for h in heads]
    vg = [slice(h * RET_V_DIM, (h + 1) * RET_V_DIM) for h in heads]
    v = [p_ref[:, RET_V_COL0 + vg[h].start:RET_V_COL0 + vg[h].stop] for h in heads]
    scores = [lax.dot_general(p_ref[:, qk[h]],
                              p_ref[:, RET_K_COL0 + qk[h].start:RET_K_COL0 + qk[h].stop],
                              NT_DIMS, preferred_element_type=F32) for h in heads]
    outs = []
    for h in heads:
        lhs = jnp.concatenate(
            [(scores[h] * mask_ref[h]).astype(BF16),
             p_ref[:, RET_QDEC_COL0 + qk[h].start:RET_QDEC_COL0 + qk[h].stop]], axis=1)
        rhs = jnp.concatenate([v[h], state_ref[h].astype(BF16)], axis=0)
        outs.append(jnp.dot(lhs, rhs, preferred_element_type=F32))
    for h in heads:
        update = lax.dot_general(
            p_ref[:, RET_KDEC_COL0 + qk[h].start:RET_KDEC_COL0 + qk[h].stop], v[h],
            TN_DIMS, preferred_element_type=F32)
        state_ref[h] = state_ref[h] * bd_ref[h] + update
    for h in heads:
        o = outs[h]
        gate = p_ref[:, RET_G_COL0 + vg[h].start:RET_G_COL0 + vg[h].stop].astype(F32)
        o_ref[:, vg[h]] = (o * _rms_scale(o) * gain_ref[h] * gate).astype(o_ref.dtype)


def _retention(x, norm_gain, w_in, out_gain, batch, seq):
    t = x.shape[0]
    nblk = seq // RET_BLOCK
    mask, q_decay, k_decay, block_decay, cos, sin = _retention_tables(seq)
    proj = _ret_proj(x, norm_gain, w_in, cos, sin, q_decay, k_decay, seq)
    whole = lambda b, n: (0, 0, 0)
    return pl.pallas_call(
        _retention_kernel,
        grid=(batch, nblk),
        in_specs=[
            pl.BlockSpec(memory_space=pltpu.SMEM),
            pl.BlockSpec((RET_BLOCK, RET_PROJ_WIDTH), lambda b, n: (b * nblk + n, 0)),
            pl.BlockSpec((RET_HEADS, RET_BLOCK, RET_BLOCK), whole),
            pl.BlockSpec((RET_HEADS, 1, RET_V_DIM), whole),
        ],
        out_specs=pl.BlockSpec((RET_BLOCK, RET_HEADS * RET_V_DIM),
                               lambda b, n: (b * nblk + n, 0)),
        out_shape=jax.ShapeDtypeStruct((t, RET_HEADS * RET_V_DIM), BF16),
        scratch_shapes=[pltpu.VMEM((RET_HEADS, RET_QK_DIM, RET_V_DIM), F32)],
        compiler_params=pltpu.CompilerParams(
            dimension_semantics=("parallel", "arbitrary"),
            vmem_limit_bytes=VMEM_LIMIT),
        name="retention",
    )(block_decay, proj, mask, out_gain.reshape(RET_HEADS, 1, RET_V_DIM))


def _sb_kernel(q_ref, k_ref, v_ref, o_ref, qs_ref, vt_ref, acc_ref, carry_ref):
    seq = q_ref.shape[0]
    nblk = seq // SB_BLOCK
    npair = q_ref.shape[1] // LANES
    per_block = LANES // SB_HEAD_DIM
    heads = range(per_block * npair)

    lane = lax.broadcasted_iota(jnp.int32, (1, LANES), 1)
    for p in range(npair):
        cols = slice(p * LANES, (p + 1) * LANES)
        q = q_ref[:, cols]
        for sub in range(per_block):
            keep = jnp.where(lane // SB_HEAD_DIM == sub, 1.0, 0.0).astype(BF16)
            qs_ref[per_block * p + sub] = q * keep
        for kb in range(nblk):
            rows = slice(kb * SB_BLOCK, (kb + 1) * SB_BLOCK)
            vt_ref[p, kb] = v_ref[rows, cols].astype(F32).T.astype(BF16)

    row = lax.broadcasted_iota(jnp.int32, (SB_BLOCK, SB_BLOCK), 0)
    col = lax.broadcasted_iota(jnp.int32, (SB_BLOCK, SB_BLOCK), 1)
    causal = row < col
    tri = jnp.where(row <= col, 1.0, 0.0).astype(BF16)

    def tile_group(q0, kbs, from_diagonal):
        jobs = [(j, h) for j in range(len(kbs)) for h in heads]
        k0s = [pl.multiple_of(kb * SB_BLOCK, SB_BLOCK) for kb in kbs]
        zs = {}
        for j, h in jobs:
            p = h // per_block
            zs[j, h] = lax.dot_general(
                k_ref[pl.ds(k0s[j], SB_BLOCK), p * LANES:(p + 1) * LANES],
                qs_ref[h, pl.ds(q0, SB_BLOCK), :], NT_DIMS, preferred_element_type=F32)
        from_here = {}
        for j, h in jobs:
            z = zs[j, h]
            softplus = jnp.maximum(z, 0.0) + jnp.log(1.0 + jnp.exp2(-jnp.abs(z))) * LOG2E
            stay = jnp.where(causal, softplus, 0.0) if from_diagonal and j == 0 else softplus
            from_here[j, h] = jnp.dot(tri, stay.astype(BF16), preferred_element_type=F32)
        carries, outs = {}, {}
        for j, h in jobs:
            p, sub = divmod(h, per_block)
            total = from_here[j, h][0:1, :]
            if from_diagonal and j == 0:
                w = jnp.where(causal, jnp.exp2(zs[j, h] - from_here[j, h]), 0.0)
                carries[h] = total
            else:
                carry = carries[h] if h in carries else carry_ref[h]
                w = jnp.exp2(zs[j, h] - from_here[j, h] - carry)
                carries[h] = carry + total
            v_t = vt_ref[p, kbs[j], sub * SB_HEAD_DIM:(sub + 1) * SB_HEAD_DIM, :]
            pv = jnp.dot(v_t, w.astype(BF16), preferred_element_type=F32)
            outs[h] = pv if h not in outs else outs[h] + pv
        least = None
        for h in heads:
            out_rows = slice(h * SB_HEAD_DIM, (h + 1) * SB_HEAD_DIM)
            if from_diagonal:
                acc_ref[out_rows, :] = outs[h]
            else:
                acc_ref[out_rows, :] += outs[h]
            carry_ref[h] = carries[h]
            least = carries[h] if least is None else jnp.minimum(least, carries[h])
        return jnp.min(least)

    def finish(q0):
        o_ref[pl.ds(q0, SB_BLOCK), :] = acc_ref[...].T.astype(o_ref.dtype)

    tile_group(0, [0], True)
    finish(0)

    def q_block(qb, _):
        q0 = pl.multiple_of(qb * SB_BLOCK, SB_BLOCK)
        least = tile_group(q0, [qb, qb - 1], True)

        def more(state):
            kb, least = state
            return jnp.logical_and(kb >= 0, least < SB_DEAD_LOG2)

        def k_block(state):
            kb, _ = state
            return kb - 1, tile_group(q0, [kb], False)

        lax.while_loop(more, k_block, (qb - 2, least))
        finish(q0)
        return 0

    lax.fori_loop(1, nblk, q_block, 0)


def _stick_breaking(proj, batch, seq):
    t = proj.shape[0]
    width = SB_GROUP * SB_HEAD_DIM
    groups = SB_HEADS // SB_GROUP
    return pl.pallas_call(
        _sb_kernel,
        grid=(batch, groups),
        in_specs=[
            pl.BlockSpec((seq, width), lambda b, g: (b, g)),
            pl.BlockSpec((seq, width), lambda b, g: (b, groups + g)),
            pl.BlockSpec((seq, width), lambda b, g: (b, 2 * groups + g)),
        ],
        out_specs=pl.BlockSpec((seq, width), lambda b, g: (b, g)),
        out_shape=jax.ShapeDtypeStruct((t, SB_HEADS * SB_HEAD_DIM), BF16),
        scratch_shapes=[
            pltpu.VMEM((SB_GROUP, seq, LANES), BF16),
            pltpu.VMEM((width // LANES, seq // SB_BLOCK, LANES, SB_BLOCK), BF16),
            pltpu.VMEM((width, SB_BLOCK), F32),
            pltpu.VMEM((SB_GROUP, 1, SB_BLOCK), F32),
        ],
        compiler_params=pltpu.CompilerParams(
            dimension_semantics=("parallel", "parallel"),
            vmem_limit_bytes=VMEM_LIMIT),
        name="stick_breaking",
    )(proj, proj, proj)


def kernel(x, mix_norm, ffn_norm, ret_w_in, ret_out_norm, ret_w_o, sb_w_in,
           sb_q_norm, sb_k_norm, sb_w_o, ffn_w_in, ffn_w_out):
    batch, seq, d = x.shape
    assert d == D_MODEL and seq % RET_BLOCK == 0 and seq % SB_BLOCK == 0
    assert (batch * seq) % PROJ_ROW_TILE == 0 and (batch * seq) % FFN_ROW_TILE == 0
    xt = x.reshape(batch * seq, d)
    for i in range(DEPTH):
        j = i // N_MIXERS
        if i % N_MIXERS == 0:
            mixed = _retention(xt, mix_norm[i], ret_w_in[j].astype(BF16), ret_out_norm[j],
                               batch, seq)
            w_o = ret_w_o[j]
        else:
            head_gain = jnp.concatenate([
                jnp.tile(sb_q_norm[j] * (LOG2E * SB_HEAD_DIM ** -0.5), SB_HEADS),
                jnp.tile(sb_k_norm[j], SB_HEADS)])
            proj = _norm_matmul(xt, mix_norm[i], sb_w_in[j].astype(BF16), head_gain)
            mixed = _stick_breaking(proj, batch, seq)
            w_o = sb_w_o[j]
        xt = _out_ffn(mixed, w_o.astype(BF16), xt, ffn_norm[i],
                      ffn_w_in[i].astype(BF16), ffn_w_out[i].astype(BF16))
    return xt.reshape(batch, seq, d)
```

```python
import functools
import math

import jax
import jax.numpy as jnp
from jax import lax
from jax.experimental import pallas as pl
from jax.experimental.pallas import tpu as pltpu

D_MODEL = 1024
DEPTH = 4
N_MIXERS = 2
CHUNK = 64
RET_HEADS = 4
RET_QK_DIM = 256
RET_V_DIM = 512
SB_HEADS = 16
SB_HEAD_DIM = 64
FFN_HIDDEN = 2816
ROPE_THETA = 10000.0
EPS = 1e-6

LANES = 128
MXU_DIM = 256
VMEM_LIMIT = 56 * 1024 * 1024

RET_BLOCK = MXU_DIM
SB_BLOCK = MXU_DIM
SB_GROUP = 8
SB_DEAD_LOG2 = 160.0
PROJ_ROW_TILE = 512
PROJ_COL = 512
FFN_ROW_TILE = 512
FFN_COL = MXU_DIM

F32 = jnp.float32
BF16 = jnp.bfloat16
NT_DIMS = (((1,), (1,)), ((), ()))
TN_DIMS = (((0,), (0,)), ((), ()))
LOG2E = math.log2(math.e)


def _rms_scale(x):
    return lax.rsqrt(jnp.mean(x * x, axis=-1, keepdims=True) + EPS)


def _resident_layer(w, layer):
    return pl.BlockSpec((None,) + w.shape[1:], lambda *_: (layer, 0, 0),
                        pipeline_mode=pl.Buffered(1))


def _head_rms_norm(y, gain):
    first = lax.broadcasted_iota(jnp.int32, y.shape, 1) < SB_HEAD_DIM
    y2 = y * y
    s0 = jnp.sum(jnp.where(first, y2, 0.0), axis=-1, keepdims=True)
    s1 = jnp.sum(jnp.where(first, 0.0, y2), axis=-1, keepdims=True)
    ms = jnp.where(first, s0, s1) * (1.0 / SB_HEAD_DIM)
    return y * lax.rsqrt(ms + EPS) * gain


def _norm_matmul_kernel(x_ref, g_ref, w_ref, hg_ref, o_ref, h_ref, *, head_norm_cols):
    x = x_ref[...]
    h_ref[...] = (x * _rms_scale(x) * g_ref[...]).astype(BF16)
    for c in range(w_ref.shape[1] // PROJ_COL):
        lo = c * PROJ_COL
        y = jnp.dot(h_ref[...], w_ref[:, lo:lo + PROJ_COL], preferred_element_type=F32)
        if lo < head_norm_cols:
            for b in range(PROJ_COL // LANES):
                cols = slice(lo + b * LANES, lo + (b + 1) * LANES)
                o_ref[:, cols] = _head_rms_norm(
                    y[:, b * LANES:(b + 1) * LANES], hg_ref[:, cols]).astype(o_ref.dtype)
        else:
            o_ref[:, lo:lo + PROJ_COL] = y.astype(o_ref.dtype)


def _norm_matmul(x, gain, w, layer, head_gain=None):
    t, d = x.shape
    n = w.shape[2]
    head_norm_cols = 0 if head_gain is None else head_gain.shape[0]
    assert head_norm_cols % PROJ_COL == 0
    hg = jnp.ones((1, n), F32)
    if head_gain is not None:
        hg = hg.at[0, :head_norm_cols].set(head_gain)
    return pl.pallas_call(
        functools.partial(_norm_matmul_kernel, head_norm_cols=head_norm_cols),
        grid=(t // PROJ_ROW_TILE,),
        in_specs=[
            pl.BlockSpec((PROJ_ROW_TILE, d), lambda i: (i, 0)),
            pl.BlockSpec((1, d), lambda i: (0, 0)),
            _resident_layer(w, layer),
            pl.BlockSpec((1, n), lambda i: (0, 0)),
        ],
        out_specs=pl.BlockSpec((PROJ_ROW_TILE, n), lambda i: (i, 0)),
        out_shape=jax.ShapeDtypeStruct((t, n), BF16),
        scratch_shapes=[pltpu.VMEM((PROJ_ROW_TILE, d), BF16)],
        compiler_params=pltpu.CompilerParams(
            dimension_semantics=("parallel",),
            vmem_limit_bytes=VMEM_LIMIT),
        name="norm_matmul",
    )(x, gain.reshape(1, d), w, hg)


def _out_ffn_kernel(a_ref, wo_ref, x_ref, g_ref, win_ref, wout_ref, o_ref, h_ref):
    x = x_ref[...] + jnp.dot(a_ref[...], wo_ref[...], preferred_element_type=F32)
    o_ref[...] = x
    h_ref[...] = (x * _rms_scale(x) * g_ref[...]).astype(BF16)
    for c in range(FFN_HIDDEN // FFN_COL):
        lo = c * FFN_COL
        h = h_ref[...]
        gate = jnp.dot(h, win_ref[:, lo:lo + FFN_COL], preferred_element_type=F32)
        up = jnp.dot(h, win_ref[:, FFN_HIDDEN + lo:FFN_HIDDEN + lo + FFN_COL],
                     preferred_element_type=F32)
        act = (gate * jax.nn.sigmoid(gate) * up).astype(BF16)
        o_ref[...] += jnp.dot(act, wout_ref[lo:lo + FFN_COL, :], preferred_element_type=F32)


def _out_ffn(a, w_o, mixer_layer, x, gain, w_in, w_out, layer):
    t, d = x.shape
    k = a.shape[1]
    rows = lambda i: (i, 0)
    whole = lambda i: (0, 0)
    return pl.pallas_call(
        _out_ffn_kernel,
        grid=(t // FFN_ROW_TILE,),
        in_specs=[
            pl.BlockSpec((FFN_ROW_TILE, k), rows),
            _resident_layer(w_o, mixer_layer),
            pl.BlockSpec((FFN_ROW_TILE, d), rows),
            pl.BlockSpec((1, d), whole),
            _resident_layer(w_in, layer),
            _resident_layer(w_out, layer),
        ],
        out_specs=pl.BlockSpec((FFN_ROW_TILE, d), rows),
        out_shape=jax.ShapeDtypeStruct((t, d), F32),
        scratch_shapes=[pltpu.VMEM((FFN_ROW_TILE, d), BF16)],
        compiler_params=pltpu.CompilerParams(
            dimension_semantics=("parallel",),
            vmem_limit_bytes=VMEM_LIMIT),
        name="out_ffn",
    )(a, w_o, x, gain.reshape(1, d), w_in, w_out)


RET_K_COL0 = RET_HEADS * RET_QK_DIM
RET_V_COL0 = 2 * RET_HEADS * RET_QK_DIM
RET_G_COL0 = RET_V_COL0 + RET_HEADS * RET_V_DIM
RET_QDEC_COL0 = RET_G_COL0 + RET_HEADS * RET_V_DIM
RET_KDEC_COL0 = RET_QDEC_COL0 + RET_HEADS * RET_QK_DIM
RET_PROJ_WIDTH = RET_KDEC_COL0 + RET_HEADS * RET_QK_DIM


def _retention_tables(seq):
    h = jnp.arange(RET_HEADS, dtype=F32)
    log_gamma = jnp.log(1.0 - 2.0 ** (-5.0 - h))[:, None, None]
    p = jnp.arange(RET_BLOCK)
    pf = p.astype(F32)
    diff = pf[:, None] - pf[None, :]
    same = (p[:, None] // CHUNK) == (p[None, :] // CHUNK)
    earlier = (p[:, None] // CHUNK) > (p[None, :] // CHUNK)
    k_scale = RET_QK_DIM ** -0.5
    mask = k_scale * jnp.where(same[None], jnp.exp(log_gamma * jnp.abs(diff)[None]),
                               jnp.where(earlier[None], jnp.exp(log_gamma * diff[None]), 0.0))
    tile_pos = (jnp.arange(PROJ_ROW_TILE) % RET_BLOCK).astype(F32)[None, :, None]
    q_decay = jnp.exp(log_gamma * (tile_pos + 1.0))
    k_decay = k_scale * jnp.exp(log_gamma * (RET_BLOCK - 1.0 - tile_pos))
    q_decay = jnp.broadcast_to(q_decay, (RET_HEADS, PROJ_ROW_TILE, LANES))
    k_decay = jnp.broadcast_to(k_decay, (RET_HEADS, PROJ_ROW_TILE, LANES))
    block_decay = jnp.exp(log_gamma[:, 0, 0] * RET_BLOCK)

    half = RET_QK_DIM // 2
    inv_freq = 1.0 / (ROPE_THETA ** (jnp.arange(0, half, dtype=F32) / half))
    ang = jnp.arange(seq).astype(F32)[:, None] * inv_freq[None, :]
    return mask, q_decay, k_decay, block_decay, jnp.cos(ang), jnp.sin(ang)


def _ret_layer_kernel(bd_ref, x_ref, g_ref, w_ref, cos_ref, sin_ref, qd_ref, kd_ref,
                      mask_ref, gain_ref, o_ref, h_ref, proj_ref, state_ref, *, tiles_per_seq):
    s = pl.program_id(0)
    cur = s % 2
    prev = 1 - cur

    @pl.when(s == 0)
    def _():
        proj_ref[1] = jnp.zeros(proj_ref.shape[1:], proj_ref.dtype)

    @pl.when(jnp.logical_or(s == 0, (s - 1) % tiles_per_seq == 0))
    def _():
        state_ref[...] = jnp.zeros_like(state_ref)

    x = x_ref[...]
    h_ref[...] = (x * _rms_scale(x) * g_ref[...]).astype(BF16)
    cos = cos_ref[...]
    sin = sin_ref[...]
    half = RET_QK_DIM // 2

    def project(c):
        lo = c * PROJ_COL
        y = jnp.dot(h_ref[...], w_ref[:, lo:lo + PROJ_COL], preferred_element_type=F32)
        if lo < RET_V_COL0:
            is_q = lo < RET_K_COL0
            dec_ref = qd_ref if is_q else kd_ref
            dec_col0 = RET_QDEC_COL0 if is_q else RET_KDEC_COL0
            rel = lo - (0 if is_q else RET_K_COL0)
            for hh in range(PROJ_COL // RET_QK_DIM):
                head = rel // RET_QK_DIM + hh
                y1 = y[:, hh * RET_QK_DIM:hh * RET_QK_DIM + half]
                y2 = y[:, hh * RET_QK_DIM + half:(hh + 1) * RET_QK_DIM]
                decay = dec_ref[head]
                for part, r in enumerate((y1 * cos - y2 * sin, y1 * sin + y2 * cos)):
                    off = hh * RET_QK_DIM + part * half
                    proj_ref[cur, :, lo + off:lo + off + half] = r.astype(BF16)
                    proj_ref[cur, :, dec_col0 + rel + off:dec_col0 + rel + off + half] = (
                        r * decay).astype(BF16)
        elif lo < RET_G_COL0:
            proj_ref[cur, :, lo:lo + PROJ_COL] = y.astype(BF16)
        else:
            half_g = 0.5 * y
            proj_ref[cur, :, lo:lo + PROJ_COL] = (half_g + half_g * jnp.tanh(half_g)).astype(BF16)

    heads = range(RET_HEADS)
    qk = [slice(h * RET_QK_DIM, (h + 1) * RET_QK_DIM) for h in heads]
    vg = [slice(h * RET_V_DIM, (h + 1) * RET_V_DIM) for h in heads]

    def core_stages(rows):
        def cols(col0, sl):
            return proj_ref[prev, rows, col0 + sl.start:col0 + sl.stop]

        box = {}

        def scores():
            box["scores"] = [lax.dot_general(cols(0, qk[h]), cols(RET_K_COL0, qk[h]), NT_DIMS,
                                             preferred_element_type=F32) for h in heads]

        def outputs():
            box["outs"] = [jnp.dot(
                jnp.concatenate([(box["scores"][h] * mask_ref[h]).astype(BF16),
                                 cols(RET_QDEC_COL0, qk[h])], axis=1),
                jnp.concatenate([cols(RET_V_COL0, vg[h]), state_ref[h].astype(BF16)], axis=0),
                preferred_element_type=F32) for h in heads]

        def update():
            for h in heads:
                upd = lax.dot_general(cols(RET_KDEC_COL0, qk[h]), cols(RET_V_COL0, vg[h]),
                                      TN_DIMS, preferred_element_type=F32)
                state_ref[h] = state_ref[h] * bd_ref[h] + upd

        def finish():
            for h in heads:
                o = box["outs"][h]
                gate = cols(RET_G_COL0, vg[h]).astype(F32)
                o_ref[rows, vg[h]] = (o * _rms_scale(o) * gain_ref[h] * gate).astype(o_ref.dtype)

        return [scores, outputs, update, finish]

    stages = [stage for b in range(PROJ_ROW_TILE // RET_BLOCK)
              for stage in core_stages(slice(b * RET_BLOCK, (b + 1) * RET_BLOCK))]
    chunk = lambda col0, n: [col0 // PROJ_COL + i for i in range(n)]
    per_part = RET_HEADS * RET_QK_DIM // PROJ_COL
    chunks = (chunk(RET_G_COL0, 2 * per_part) + chunk(0, per_part)
              + chunk(RET_K_COL0, per_part) + chunk(RET_V_COL0, 2 * per_part))
    assert sorted(chunks) == list(range(w_ref.shape[1] // PROJ_COL))
    s0, o0, u0, f0, s1, o1, u1, f1 = stages
    plan = [s0, s1, o0, u0, 2, f0, 2, o1, 2, u1, 2, f1, 4]
    done = 0
    for item in plan:
        if callable(item):
            item()
        else:
            for c in chunks[done:done + item]:
                project(c)
            done += item
    assert done == len(chunks)


def _retention(x, norm_gain, w_in, layer, out_gain, batch, seq):
    t, d = x.shape
    tiles = t // PROJ_ROW_TILE
    tiles_per_seq = seq // PROJ_ROW_TILE
    mask, q_decay, k_decay, block_decay, cos, sin = _retention_tables(seq)
    this = lambda s: jnp.minimum(s, tiles - 1)
    last = lambda s: jnp.maximum(s - 1, 0)
    whole2 = lambda s: (0, 0)
    whole3 = lambda s: (0, 0, 0)
    pos = lambda s: (this(s) % tiles_per_seq, 0)
    once = pl.Buffered(1)
    return pl.pallas_call(
        functools.partial(_ret_layer_kernel, tiles_per_seq=tiles_per_seq),
        grid=(tiles + 1,),
        in_specs=[
            pl.BlockSpec(memory_space=pltpu.SMEM),
            pl.BlockSpec((PROJ_ROW_TILE, d), lambda s: (this(s), 0)),
            pl.BlockSpec((1, d), whole2),
            _resident_layer(w_in, layer),
            pl.BlockSpec((PROJ_ROW_TILE, RET_QK_DIM // 2), pos),
            pl.BlockSpec((PROJ_ROW_TILE, RET_QK_DIM // 2), pos),
            pl.BlockSpec((RET_HEADS, PROJ_ROW_TILE, LANES), whole3, pipeline_mode=once),
            pl.BlockSpec((RET_HEADS, PROJ_ROW_TILE, LANES), whole3, pipeline_mode=once),
            pl.BlockSpec((RET_HEADS, RET_BLOCK, RET_BLOCK), whole3, pipeline_mode=once),
            pl.BlockSpec((RET_HEADS, 1, RET_V_DIM), whole3),
        ],
        out_specs=pl.BlockSpec((PROJ_ROW_TILE, RET_HEADS * RET_V_DIM), lambda s: (last(s), 0)),
        out_shape=jax.ShapeDtypeStruct((t, RET_HEADS * RET_V_DIM), BF16),
        scratch_shapes=[
            pltpu.VMEM((PROJ_ROW_TILE, d), BF16),
            pltpu.VMEM((2, PROJ_ROW_TILE, RET_PROJ_WIDTH), BF16),
            pltpu.VMEM((RET_HEADS, RET_QK_DIM, RET_V_DIM), F32),
        ],
        compiler_params=pltpu.CompilerParams(
            dimension_semantics=("arbitrary",),
            vmem_limit_bytes=VMEM_LIMIT),
        name="retention_layer",
    )(block_decay, x, norm_gain.reshape(1, d), w_in, cos, sin, q_decay, k_decay, mask,
      out_gain.reshape(RET_HEADS, 1, RET_V_DIM))


def _sb_kernel(q_ref, k_ref, v_ref, o_ref, qs_ref, vt_ref, acc_ref, carry_ref):
    seq = q_ref.shape[0]
    nblk = seq // SB_BLOCK
    npair = q_ref.shape[1] // LANES
    per_block = LANES // SB_HEAD_DIM
    heads = range(per_block * npair)

    lane = lax.broadcasted_iota(jnp.int32, (1, LANES), 1)
    for p in range(npair):
        cols = slice(p * LANES, (p + 1) * LANES)
        q = q_ref[:, cols]
        for sub in range(per_block):
            keep = jnp.where(lane // SB_HEAD_DIM == sub, 1.0, 0.0).astype(BF16)
            qs_ref[per_block * p + sub] = q * keep
        for kb in range(nblk):
            rows = slice(kb * SB_BLOCK, (kb + 1) * SB_BLOCK)
            vt_ref[p, kb] = v_ref[rows, cols].astype(F32).T.astype(BF16)

    row = lax.broadcasted_iota(jnp.int32, (SB_BLOCK, SB_BLOCK), 0)
    col = lax.broadcasted_iota(jnp.int32, (SB_BLOCK, SB_BLOCK), 1)
    causal = row < col
    tri = jnp.where(row <= col, 1.0, 0.0).astype(BF16)

    def tile_group(q0, kbs, from_diagonal):
        jobs = [(j, h) for j in range(len(kbs)) for h in heads]
        k0s = [pl.multiple_of(kb * SB_BLOCK, SB_BLOCK) for kb in kbs]
        zs = {}
        for j, h in jobs:
            p = h // per_block
            zs[j, h] = lax.dot_general(
                k_ref[pl.ds(k0s[j], SB_BLOCK), p * LANES:(p + 1) * LANES],
                qs_ref[h, pl.ds(q0, SB_BLOCK), :], NT_DIMS, preferred_element_type=F32)
        from_here = {}
        for j, h in jobs:
            z = zs[j, h]
            softplus = jnp.maximum(z, 0.0) + jnp.log(1.0 + jnp.exp2(-jnp.abs(z))) * LOG2E
            stay = jnp.where(causal, softplus, 0.0) if from_diagonal and j == 0 else softplus
            from_here[j, h] = jnp.dot(tri, stay.astype(BF16), preferred_element_type=F32)
        carries, outs = {}, {}
        for j, h in jobs:
            p, sub = divmod(h, per_block)
            total = from_here[j, h][0:1, :]
            if from_diagonal and j == 0:
                w = jnp.where(causal, jnp.exp2(zs[j, h] - from_here[j, h]), 0.0)
                carries[h] = total
            else:
                carry = carries[h] if h in carries else carry_ref[h]
                w = jnp.exp2(zs[j, h] - from_here[j, h] - carry)
                carries[h] = carry + total
            v_t = vt_ref[p, kbs[j], sub * SB_HEAD_DIM:(sub + 1) * SB_HEAD_DIM, :]
            pv = jnp.dot(v_t, w.astype(BF16), preferred_element_type=F32)
            outs[h] = pv if h not in outs else outs[h] + pv
        least = None
        for h in heads:
            out_rows = slice(h * SB_HEAD_DIM, (h + 1) * SB_HEAD_DIM)
            if from_diagonal:
                acc_ref[out_rows, :] = outs[h]
            else:
                acc_ref[out_rows, :] += outs[h]
            carry_ref[h] = carries[h]
            least = carries[h] if least is None else jnp.minimum(least, carries[h])
        return jnp.min(least)

    def finish(q0):
        o_ref[pl.ds(q0, SB_BLOCK), :] = acc_ref[...].T.astype(o_ref.dtype)

    tile_group(0, [0], True)
    finish(0)

    def q_block(qb, _):
        q0 = pl.multiple_of(qb * SB_BLOCK, SB_BLOCK)
        least = tile_group(q0, [qb, qb - 1], True)

        def more(state):
            kb, least = state
            return jnp.logical_and(kb >= 0, least < SB_DEAD_LOG2)

        def k_block(state):
            kb, _ = state
            return kb - 1, tile_group(q0, [kb], False)

        lax.while_loop(more, k_block, (qb - 2, least))
        finish(q0)
        return 0

    lax.fori_loop(1, nblk, q_block, 0)


def _stick_breaking(proj, batch, seq):
    t = proj.shape[0]
    width = SB_GROUP * SB_HEAD_DIM
    groups = SB_HEADS // SB_GROUP
    return pl.pallas_call(
        _sb_kernel,
        grid=(batch, groups),
        in_specs=[
            pl.BlockSpec((seq, width), lambda b, g: (b, g)),
            pl.BlockSpec((seq, width), lambda b, g: (b, groups + g)),
            pl.BlockSpec((seq, width), lambda b, g: (b, 2 * groups + g)),
        ],
        out_specs=pl.BlockSpec((seq, width), lambda b, g: (b, g)),
        out_shape=jax.ShapeDtypeStruct((t, SB_HEADS * SB_HEAD_DIM), BF16),
        scratch_shapes=[
            pltpu.VMEM((SB_GROUP, seq, LANES), BF16),
            pltpu.VMEM((width // LANES, seq // SB_BLOCK, LANES, SB_BLOCK), BF16),
            pltpu.VMEM((width, SB_BLOCK), F32),
            pltpu.VMEM((SB_GROUP, 1, SB_BLOCK), F32),
        ],
        compiler_params=pltpu.CompilerParams(
            dimension_semantics=("parallel", "parallel"),
            vmem_limit_bytes=VMEM_LIMIT),
        name="stick_breaking",
    )(proj, proj, proj)


def kernel(x, mix_norm, ffn_norm, ret_w_in, ret_out_norm, ret_w_o, sb_w_in,
           sb_q_norm, sb_k_norm, sb_w_o, ffn_w_in, ffn_w_out):
    batch, seq, d = x.shape
    assert d == D_MODEL and seq % RET_BLOCK == 0 and seq % SB_BLOCK == 0
    assert (batch * seq) % PROJ_ROW_TILE == 0 and (batch * seq) % FFN_ROW_TILE == 0
    xt = x.reshape(batch * seq, d)
    ret_w_in, ret_w_o, sb_w_in, sb_w_o, ffn_w_in, ffn_w_out = (
        w.astype(BF16) for w in (ret_w_in, ret_w_o, sb_w_in, sb_w_o, ffn_w_in, ffn_w_out))
    for i in range(DEPTH):
        j = i // N_MIXERS
        if i % N_MIXERS == 0:
            mixed = _retention(xt, mix_norm[i], ret_w_in, j, ret_out_norm[j], batch, seq)
            w_o = ret_w_o
        else:
            head_gain = jnp.concatenate([
                jnp.tile(sb_q_norm[j] * (LOG2E * SB_HEAD_DIM ** -0.5), SB_HEADS),
                jnp.tile(sb_k_norm[j], SB_HEADS)])
            proj = _norm_matmul(xt, mix_norm[i], sb_w_in, j, head_gain)
            mixed = _stick_breaking(proj, batch, seq)
            w_o = sb_w_o
        xt = _out_ffn(mixed, w_o, j, xt, ffn_norm[i], ffn_w_in, ffn_w_out, i)
    return xt.reshape(batch, seq, d)
```

```python
import functools
import math

import jax
import jax.numpy as jnp
from jax import lax
from jax.experimental import pallas as pl
from jax.experimental.pallas import tpu as pltpu

D_MODEL = 1024
DEPTH = 4
N_MIXERS = 2
CHUNK = 64
RET_HEADS = 4
RET_QK_DIM = 256
RET_V_DIM = 512
SB_HEADS = 16
SB_HEAD_DIM = 64
FFN_HIDDEN = 2816
ROPE_THETA = 10000.0
EPS = 1e-6

LANES = 128
MXU_DIM = 256
VMEM_LIMIT = 56 * 1024 * 1024

RET_BLOCK = MXU_DIM
SB_BLOCK = MXU_DIM
SB_GROUP = 8
SB_DEAD_LOG2 = 160.0
PROJ_ROW_TILE = 512
PROJ_COL = 512
FFN_ROW_TILE = 1024
FFN_COL = MXU_DIM

F32 = jnp.float32
BF16 = jnp.bfloat16
NT_DIMS = (((1,), (1,)), ((), ()))
TN_DIMS = (((0,), (0,)), ((), ()))
LOG2E = math.log2(math.e)


def _rms_scale(x):
    return lax.rsqrt(jnp.mean(x * x, axis=-1, keepdims=True) + EPS)


def _resident_layer(w, layer):
    return pl.BlockSpec((None,) + w.shape[1:], lambda *_: (layer, 0, 0),
                        pipeline_mode=pl.Buffered(1))


def _head_rms_norm(y, gain):
    first = lax.broadcasted_iota(jnp.int32, y.shape, 1) < SB_HEAD_DIM
    y2 = y * y
    s0 = jnp.sum(jnp.where(first, y2, 0.0), axis=-1, keepdims=True)
    s1 = jnp.sum(jnp.where(first, 0.0, y2), axis=-1, keepdims=True)
    ms = jnp.where(first, s0, s1) * (1.0 / SB_HEAD_DIM)
    return y * lax.rsqrt(ms + EPS) * gain


def _norm_matmul_kernel(x_ref, g_ref, w_ref, hg_ref, o_ref, h_ref, *, head_norm_cols):
    x = x_ref[...]
    h_ref[...] = (x * _rms_scale(x) * g_ref[...]).astype(BF16)
    for c in range(w_ref.shape[1] // PROJ_COL):
        lo = c * PROJ_COL
        y = jnp.dot(h_ref[...], w_ref[:, lo:lo + PROJ_COL], preferred_element_type=F32)
        if lo < head_norm_cols:
            for b in range(PROJ_COL // LANES):
                cols = slice(lo + b * LANES, lo + (b + 1) * LANES)
                o_ref[:, cols] = _head_rms_norm(
                    y[:, b * LANES:(b + 1) * LANES], hg_ref[:, cols]).astype(o_ref.dtype)
        else:
            o_ref[:, lo:lo + PROJ_COL] = y.astype(o_ref.dtype)


def _norm_matmul(x, gain, w, layer, head_gain=None):
    t, d = x.shape
    n = w.shape[2]
    head_norm_cols = 0 if head_gain is None else head_gain.shape[0]
    assert head_norm_cols % PROJ_COL == 0
    hg = jnp.ones((1, n), F32)
    if head_gain is not None:
        hg = hg.at[0, :head_norm_cols].set(head_gain)
    return pl.pallas_call(
        functools.partial(_norm_matmul_kernel, head_norm_cols=head_norm_cols),
        grid=(t // PROJ_ROW_TILE,),
        in_specs=[
            pl.BlockSpec((PROJ_ROW_TILE, d), lambda i: (i, 0)),
            pl.BlockSpec((1, d), lambda i: (0, 0)),
            _resident_layer(w, layer),
            pl.BlockSpec((1, n), lambda i: (0, 0)),
        ],
        out_specs=pl.BlockSpec((PROJ_ROW_TILE, n), lambda i: (i, 0)),
        out_shape=jax.ShapeDtypeStruct((t, n), BF16),
        scratch_shapes=[pltpu.VMEM((PROJ_ROW_TILE, d), BF16)],
        compiler_params=pltpu.CompilerParams(
            dimension_semantics=("parallel",),
            vmem_limit_bytes=VMEM_LIMIT),
        name="norm_matmul",
    )(x, gain.reshape(1, d), w, hg)


def _out_ffn_kernel(a_ref, wo_ref, x_ref, g_ref, win_ref, wout_ref, o_ref, h_ref):
    x = x_ref[...] + jnp.dot(a_ref[...], wo_ref[...], preferred_element_type=F32)
    o_ref[...] = x
    h_ref[...] = (x * _rms_scale(x) * g_ref[...]).astype(BF16)
    for c in range(FFN_HIDDEN // FFN_COL):
        lo = c * FFN_COL
        h = h_ref[...]
        gate = jnp.dot(h, win_ref[:, lo:lo + FFN_COL], preferred_element_type=F32)
        up = jnp.dot(h, win_ref[:, FFN_HIDDEN + lo:FFN_HIDDEN + lo + FFN_COL],
                     preferred_element_type=F32)
        act = (gate * jax.nn.sigmoid(gate) * up).astype(BF16)
        o_ref[...] += jnp.dot(act, wout_ref[lo:lo + FFN_COL, :], preferred_element_type=F32)


def _out_ffn(a, w_o, mixer_layer, x, gain, w_in, w_out, layer):
    t, d = x.shape
    k = a.shape[1]
    rows = lambda i: (i, 0)
    whole = lambda i: (0, 0)
    return pl.pallas_call(
        _out_ffn_kernel,
        grid=(t // FFN_ROW_TILE,),
        in_specs=[
            pl.BlockSpec((FFN_ROW_TILE, k), rows),
            _resident_layer(w_o, mixer_layer),
            pl.BlockSpec((FFN_ROW_TILE, d), rows),
            pl.BlockSpec((1, d), whole),
            _resident_layer(w_in, layer),
            _resident_layer(w_out, layer),
        ],
        out_specs=pl.BlockSpec((FFN_ROW_TILE, d), rows),
        out_shape=jax.ShapeDtypeStruct((t, d), F32),
        scratch_shapes=[pltpu.VMEM((FFN_ROW_TILE, d), BF16)],
        compiler_params=pltpu.CompilerParams(
            dimension_semantics=("parallel",),
            vmem_limit_bytes=VMEM_LIMIT),
        name="out_ffn",
    )(a, w_o, x, gain.reshape(1, d), w_in, w_out)


RET_K_COL0 = RET_HEADS * RET_QK_DIM
RET_V_COL0 = 2 * RET_HEADS * RET_QK_DIM
RET_G_COL0 = RET_V_COL0 + RET_HEADS * RET_V_DIM
RET_QDEC_COL0 = RET_G_COL0 + RET_HEADS * RET_V_DIM
RET_KDEC_COL0 = RET_QDEC_COL0 + RET_HEADS * RET_QK_DIM
RET_PROJ_WIDTH = RET_KDEC_COL0 + RET_HEADS * RET_QK_DIM


def _retention_tables(seq):
    h = jnp.arange(RET_HEADS, dtype=F32)
    log_gamma = jnp.log(1.0 - 2.0 ** (-5.0 - h))[:, None, None]
    p = jnp.arange(RET_BLOCK)
    pf = p.astype(F32)
    diff = pf[:, None] - pf[None, :]
    same = (p[:, None] // CHUNK) == (p[None, :] // CHUNK)
    earlier = (p[:, None] // CHUNK) > (p[None, :] // CHUNK)
    k_scale = RET_QK_DIM ** -0.5
    mask = k_scale * jnp.where(same[None], jnp.exp(log_gamma * jnp.abs(diff)[None]),
                               jnp.where(earlier[None], jnp.exp(log_gamma * diff[None]), 0.0))
    tile_pos = (jnp.arange(PROJ_ROW_TILE) % RET_BLOCK).astype(F32)[None, :, None]
    q_decay = jnp.exp(log_gamma * (tile_pos + 1.0))
    k_decay = k_scale * jnp.exp(log_gamma * (RET_BLOCK - 1.0 - tile_pos))
    q_decay = jnp.broadcast_to(q_decay, (RET_HEADS, PROJ_ROW_TILE, LANES))
    k_decay = jnp.broadcast_to(k_decay, (RET_HEADS, PROJ_ROW_TILE, LANES))
    block_decay = jnp.exp(log_gamma[:, 0, 0] * RET_BLOCK)

    half = RET_QK_DIM // 2
    inv_freq = 1.0 / (ROPE_THETA ** (jnp.arange(0, half, dtype=F32) / half))
    ang = jnp.arange(seq).astype(F32)[:, None] * inv_freq[None, :]
    return mask, q_decay, k_decay, block_decay, jnp.cos(ang), jnp.sin(ang)


def _ret_layer_kernel(bd_ref, x_ref, g_ref, w_ref, cos_ref, sin_ref, qd_ref, kd_ref,
                      mask_ref, gain_ref, o_ref, h_ref, proj_ref, state_ref, *, tiles_per_seq):
    s = pl.program_id(0)
    cur = s % 2
    prev = 1 - cur

    @pl.when(s == 0)
    def _():
        proj_ref[1] = jnp.zeros(proj_ref.shape[1:], proj_ref.dtype)

    @pl.when(jnp.logical_or(s == 0, (s - 1) % tiles_per_seq == 0))
    def _():
        state_ref[...] = jnp.zeros_like(state_ref)

    x = x_ref[...]
    h_ref[...] = (x * _rms_scale(x) * g_ref[...]).astype(BF16)
    cos = cos_ref[...]
    sin = sin_ref[...]
    half = RET_QK_DIM // 2

    def project(c):
        lo = c * PROJ_COL
        y = jnp.dot(h_ref[...], w_ref[:, lo:lo + PROJ_COL], preferred_element_type=F32)
        if lo < RET_V_COL0:
            is_q = lo < RET_K_COL0
            dec_ref = qd_ref if is_q else kd_ref
            dec_col0 = RET_QDEC_COL0 if is_q else RET_KDEC_COL0
            rel = lo - (0 if is_q else RET_K_COL0)
            for hh in range(PROJ_COL // RET_QK_DIM):
                head = rel // RET_QK_DIM + hh
                y1 = y[:, hh * RET_QK_DIM:hh * RET_QK_DIM + half]
                y2 = y[:, hh * RET_QK_DIM + half:(hh + 1) * RET_QK_DIM]
                decay = dec_ref[head]
                for part, r in enumerate((y1 * cos - y2 * sin, y1 * sin + y2 * cos)):
                    off = hh * RET_QK_DIM + part * half
                    proj_ref[cur, :, lo + off:lo + off + half] = r.astype(BF16)
                    proj_ref[cur, :, dec_col0 + rel + off:dec_col0 + rel + off + half] = (
                        r * decay).astype(BF16)
        elif lo < RET_G_COL0:
            proj_ref[cur, :, lo:lo + PROJ_COL] = y.astype(BF16)
        else:
            half_g = 0.5 * y
            proj_ref[cur, :, lo:lo + PROJ_COL] = (half_g + half_g * jnp.tanh(half_g)).astype(BF16)

    heads = range(RET_HEADS)
    qk = [slice(h * RET_QK_DIM, (h + 1) * RET_QK_DIM) for h in heads]
    vg = [slice(h * RET_V_DIM, (h + 1) * RET_V_DIM) for h in heads]

    def core_stages(rows):
        def cols(col0, sl):
            return proj_ref[prev, rows, col0 + sl.start:col0 + sl.stop]

        box = {}

        def scores():
            box["scores"] = [lax.dot_general(cols(0, qk[h]), cols(RET_K_COL0, qk[h]), NT_DIMS,
                                             preferred_element_type=F32) for h in heads]

        def outputs():
            box["outs"] = [jnp.dot(
                jnp.concatenate([(box["scores"][h] * mask_ref[h]).astype(BF16),
                                 cols(RET_QDEC_COL0, qk[h])], axis=1),
                jnp.concatenate([cols(RET_V_COL0, vg[h]), state_ref[h].astype(BF16)], axis=0),
                preferred_element_type=F32) for h in heads]

        def update():
            for h in heads:
                upd = lax.dot_general(cols(RET_KDEC_COL0, qk[h]), cols(RET_V_COL0, vg[h]),
                                      TN_DIMS, preferred_element_type=F32)
                state_ref[h] = state_ref[h] * bd_ref[h] + upd

        def finish():
            for h in heads:
                o = box["outs"][h]
                gate = cols(RET_G_COL0, vg[h]).astype(F32)
                o_ref[rows, vg[h]] = (o * _rms_scale(o) * gain_ref[h] * gate).astype(o_ref.dtype)

        return [scores, outputs, update, finish]

    stages = [stage for b in range(PROJ_ROW_TILE // RET_BLOCK)
              for stage in core_stages(slice(b * RET_BLOCK, (b + 1) * RET_BLOCK))]
    chunk = lambda col0, n: [col0 // PROJ_COL + i for i in range(n)]
    per_part = RET_HEADS * RET_QK_DIM // PROJ_COL
    chunks = (chunk(RET_G_COL0, 2 * per_part) + chunk(0, per_part)
              + chunk(RET_K_COL0, per_part) + chunk(RET_V_COL0, 2 * per_part))
    assert sorted(chunks) == list(range(w_ref.shape[1] // PROJ_COL))
    s0, o0, u0, f0, s1, o1, u1, f1 = stages
    plan = [s0, s1, o0, u0, 2, f0, 2, o1, 2, u1, 2, f1, 4]
    done = 0
    for item in plan:
        if callable(item):
            item()
        else:
            for c in chunks[done:done + item]:
                project(c)
            done += item
    assert done == len(chunks)


def _retention(x, norm_gain, w_in, layer, out_gain, batch, seq):
    t, d = x.shape
    tiles = t // PROJ_ROW_TILE
    tiles_per_seq = seq // PROJ_ROW_TILE
    mask, q_decay, k_decay, block_decay, cos, sin = _retention_tables(seq)
    this = lambda s: jnp.minimum(s, tiles - 1)
    last = lambda s: jnp.maximum(s - 1, 0)
    whole2 = lambda s: (0, 0)
    whole3 = lambda s: (0, 0, 0)
    pos = lambda s: (this(s) % tiles_per_seq, 0)
    once = pl.Buffered(1)
    return pl.pallas_call(
        functools.partial(_ret_layer_kernel, tiles_per_seq=tiles_per_seq),
        grid=(tiles + 1,),
        in_specs=[
            pl.BlockSpec(memory_space=pltpu.SMEM),
            pl.BlockSpec((PROJ_ROW_TILE, d), lambda s: (this(s), 0)),
            pl.BlockSpec((1, d), whole2),
            _resident_layer(w_in, layer),
            pl.BlockSpec((PROJ_ROW_TILE, RET_QK_DIM // 2), pos),
            pl.BlockSpec((PROJ_ROW_TILE, RET_QK_DIM // 2), pos),
            pl.BlockSpec((RET_HEADS, PROJ_ROW_TILE, LANES), whole3, pipeline_mode=once),
            pl.BlockSpec((RET_HEADS, PROJ_ROW_TILE, LANES), whole3, pipeline_mode=once),
            pl.BlockSpec((RET_HEADS, RET_BLOCK, RET_BLOCK), whole3, pipeline_mode=once),
            pl.BlockSpec((RET_HEADS, 1, RET_V_DIM), whole3),
        ],
        out_specs=pl.BlockSpec((PROJ_ROW_TILE, RET_HEADS * RET_V_DIM), lambda s: (last(s), 0)),
        out_shape=jax.ShapeDtypeStruct((t, RET_HEADS * RET_V_DIM), BF16),
        scratch_shapes=[
            pltpu.VMEM((PROJ_ROW_TILE, d), BF16),
            pltpu.VMEM((2, PROJ_ROW_TILE, RET_PROJ_WIDTH), BF16),
            pltpu.VMEM((RET_HEADS, RET_QK_DIM, RET_V_DIM), F32),
        ],
        compiler_params=pltpu.CompilerParams(
            dimension_semantics=("arbitrary",),
            vmem_limit_bytes=VMEM_LIMIT),
        name="retention_layer",
    )(block_decay, x, norm_gain.reshape(1, d), w_in, cos, sin, q_decay, k_decay, mask,
      out_gain.reshape(RET_HEADS, 1, RET_V_DIM))


def _sb_kernel(q_ref, k_ref, v_ref, o_ref, qs_ref, vt_ref, acc_ref, carry_ref):
    seq = q_ref.shape[0]
    nblk = seq // SB_BLOCK
    npair = q_ref.shape[1] // LANES
    per_block = LANES // SB_HEAD_DIM
    heads = range(per_block * npair)

    lane = lax.broadcasted_iota(jnp.int32, (1, LANES), 1)
    for p in range(npair):
        cols = slice(p * LANES, (p + 1) * LANES)
        q = q_ref[:, cols]
        for sub in range(per_block):
            keep = jnp.where(lane // SB_HEAD_DIM == sub, 1.0, 0.0).astype(BF16)
            qs_ref[per_block * p + sub] = q * keep
        for kb in range(nblk):
            rows = slice(kb * SB_BLOCK, (kb + 1) * SB_BLOCK)
            vt_ref[p, kb] = v_ref[rows, cols].astype(F32).T.astype(BF16)

    row = lax.broadcasted_iota(jnp.int32, (SB_BLOCK, SB_BLOCK), 0)
    col = lax.broadcasted_iota(jnp.int32, (SB_BLOCK, SB_BLOCK), 1)
    causal = row < col
    tri = jnp.where(causal, 1.0, 0.0).astype(BF16)

    def tile_group(q0, kbs, from_diagonal):
        jobs = [(j, h) for j in range(len(kbs)) for h in heads]
        k0s = [pl.multiple_of(kb * SB_BLOCK, SB_BLOCK) for kb in kbs]
        zs = {}
        for j, h in jobs:
            p = h // per_block
            zs[j, h] = lax.dot_general(
                k_ref[pl.ds(k0s[j], SB_BLOCK), p * LANES:(p + 1) * LANES],
                qs_ref[h, pl.ds(q0, SB_BLOCK), :], NT_DIMS, preferred_element_type=F32)
        log_beta, later, totals = {}, {}, {}
        for j, h in jobs:
            z = zs[j, h]
            softplus = jnp.maximum(z, 0.0) + jnp.log(1.0 + jnp.exp2(-jnp.abs(z))) * LOG2E
            log_beta[j, h] = z - softplus
            stay = jnp.where(causal, softplus, 0.0) if from_diagonal and j == 0 else softplus
            later[j, h] = jnp.dot(tri, stay.astype(BF16), preferred_element_type=F32)
            totals[j, h] = later[j, h][0:1, :] + stay[0:1, :]
        carries, outs = {}, {}
        for j, h in jobs:
            p, sub = divmod(h, per_block)
            if from_diagonal and j == 0:
                w = jnp.where(causal, jnp.exp2(log_beta[j, h] - later[j, h]), 0.0)
                carries[h] = totals[j, h]
            else:
                carry = carries[h] if h in carries else carry_ref[h]
                w = jnp.exp2(log_beta[j, h] - later[j, h] - carry)
                carries[h] = carry + totals[j, h]
            v_t = vt_ref[p, kbs[j], sub * SB_HEAD_DIM:(sub + 1) * SB_HEAD_DIM, :]
            pv = jnp.dot(v_t, w.astype(BF16), preferred_element_type=F32)
            outs[h] = pv if h not in outs else outs[h] + pv
        least = None
        for h in heads:
            out_rows = slice(h * SB_HEAD_DIM, (h + 1) * SB_HEAD_DIM)
            if from_diagonal:
                acc_ref[out_rows, :] = outs[h]
            else:
                acc_ref[out_rows, :] += outs[h]
            carry_ref[h] = carries[h]
            least = carries[h] if least is None else jnp.minimum(least, carries[h])
        return jnp.min(least)

    def finish(q0):
        o_ref[pl.ds(q0, SB_BLOCK), :] = acc_ref[...].T.astype(o_ref.dtype)

    tile_group(0, [0], True)
    finish(0)

    def q_block(qb, _):
        q0 = pl.multiple_of(qb * SB_BLOCK, SB_BLOCK)
        least = tile_group(q0, [qb, qb - 1], True)

        def more(state):
            kb, least = state
            return jnp.logical_and(kb >= 0, least < SB_DEAD_LOG2)

        def k_block(state):
            kb, _ = state
            return kb - 1, tile_group(q0, [kb], False)

        lax.while_loop(more, k_block, (qb - 2, least))
        finish(q0)
        return 0

    lax.fori_loop(1, nblk, q_block, 0)


def _stick_breaking(proj, batch, seq):
    t = proj.shape[0]
    width = SB_GROUP * SB_HEAD_DIM
    groups = SB_HEADS // SB_GROUP
    return pl.pallas_call(
        _sb_kernel,
        grid=(batch, groups),
        in_specs=[
            pl.BlockSpec((seq, width), lambda b, g: (b, g)),
            pl.BlockSpec((seq, width), lambda b, g: (b, groups + g)),
            pl.BlockSpec((seq, width), lambda b, g: (b, 2 * groups + g)),
        ],
        out_specs=pl.BlockSpec((seq, width), lambda b, g: (b, g)),
        out_shape=jax.ShapeDtypeStruct((t, SB_HEADS * SB_HEAD_DIM), BF16),
        scratch_shapes=[
            pltpu.VMEM((SB_GROUP, seq, LANES), BF16),
            pltpu.VMEM((width // LANES, seq // SB_BLOCK, LANES, SB_BLOCK), BF16),
            pltpu.VMEM((width, SB_BLOCK), F32),
            pltpu.VMEM((SB_GROUP, 1, SB_BLOCK), F32),
        ],
        compiler_params=pltpu.CompilerParams(
            dimension_semantics=("parallel", "parallel"),
            vmem_limit_bytes=VMEM_LIMIT),
        name="stick_breaking",
    )(proj, proj, proj)


def kernel(x, mix_norm, ffn_norm, ret_w_in, ret_out_norm, ret_w_o, sb_w_in,
           sb_q_norm, sb_k_norm, sb_w_o, ffn_w_in, ffn_w_out):
    batch, seq, d = x.shape
    assert d == D_MODEL and seq % RET_BLOCK == 0 and seq % SB_BLOCK == 0
    assert (batch * seq) % PROJ_ROW_TILE == 0 and (batch * seq) % FFN_ROW_TILE == 0
    xt = x.reshape(batch * seq, d)
    ret_w_in, ret_w_o, sb_w_in, sb_w_o, ffn_w_in, ffn_w_out = (
        w.astype(BF16) for w in (ret_w_in, ret_w_o, sb_w_in, sb_w_o, ffn_w_in, ffn_w_out))
    for i in range(DEPTH):
        j = i // N_MIXERS
        if i % N_MIXERS == 0:
            mixed = _retention(xt, mix_norm[i], ret_w_in, j, ret_out_norm[j], batch, seq)
            w_o = ret_w_o
        else:
            head_gain = jnp.concatenate([
                jnp.tile(sb_q_norm[j] * (LOG2E * SB_HEAD_DIM ** -0.5), SB_HEADS),
                jnp.tile(sb_k_norm[j], SB_HEADS)])
            proj = _norm_matmul(xt, mix_norm[i], sb_w_in, j, head_gain)
            mixed = _stick_breaking(proj, batch, seq)
            w_o = sb_w_o
        xt = _out_ffn(mixed, w_o, j, xt, ffn_norm[i], ffn_w_in, ffn_w_out, i)
    return xt.reshape(batch, seq, d)
```

```python
import functools
import math

import jax
import jax.numpy as jnp
from jax import lax
from jax.experimental import pallas as pl
from jax.experimental.pallas import tpu as pltpu

D_MODEL = 1024
DEPTH = 4
N_MIXERS = 2
CHUNK = 64
RET_HEADS = 4
RET_QK_DIM = 256
RET_V_DIM = 512
SB_HEADS = 16
SB_HEAD_DIM = 64
FFN_HIDDEN = 2816
ROPE_THETA = 10000.0
EPS = 1e-6

LANES = 128
MXU_DIM = 256
VMEM_LIMIT = 56 * 1024 * 1024

RET_BLOCK = MXU_DIM
SB_BLOCK = MXU_DIM
SB_GROUP = 8
SB_DEAD_LOG2 = 160.0
PROJ_ROW_TILE = 512
PROJ_COL = 512
FFN_ROW_TILE = 1024
FFN_COL = MXU_DIM

F32 = jnp.float32
BF16 = jnp.bfloat16
NT_DIMS = (((1,), (1,)), ((), ()))
TN_DIMS = (((0,), (0,)), ((), ()))
LOG2E = math.log2(math.e)


def _rms_scale(x):
    return lax.rsqrt(jnp.mean(x * x, axis=-1, keepdims=True) + EPS)


def _resident_layer(w, layer):
    return pl.BlockSpec((None,) + w.shape[1:], lambda *_: (layer, 0, 0),
                        pipeline_mode=pl.Buffered(1))


def _head_rms_norm(y, gain):
    first = lax.broadcasted_iota(jnp.int32, y.shape, 1) < SB_HEAD_DIM
    y2 = y * y
    s0 = jnp.sum(jnp.where(first, y2, 0.0), axis=-1, keepdims=True)
    s1 = jnp.sum(jnp.where(first, 0.0, y2), axis=-1, keepdims=True)
    ms = jnp.where(first, s0, s1) * (1.0 / SB_HEAD_DIM)
    return y * lax.rsqrt(ms + EPS) * gain


def _norm_matmul_kernel(x_ref, g_ref, w_ref, hg_ref, o_ref, h_ref, *, head_norm_cols):
    x = x_ref[...]
    h_ref[...] = (x * _rms_scale(x) * g_ref[...]).astype(BF16)
    for c in range(w_ref.shape[1] // PROJ_COL):
        lo = c * PROJ_COL
        y = jnp.dot(h_ref[...], w_ref[:, lo:lo + PROJ_COL], preferred_element_type=F32)
        if lo < head_norm_cols:
            for b in range(PROJ_COL // LANES):
                cols = slice(lo + b * LANES, lo + (b + 1) * LANES)
                o_ref[:, cols] = _head_rms_norm(
                    y[:, b * LANES:(b + 1) * LANES], hg_ref[:, cols]).astype(o_ref.dtype)
        else:
            o_ref[:, lo:lo + PROJ_COL] = y.astype(o_ref.dtype)


def _norm_matmul(x, gain, w, layer, head_gain=None):
    t, d = x.shape
    n = w.shape[2]
    head_norm_cols = 0 if head_gain is None else head_gain.shape[0]
    assert head_norm_cols % PROJ_COL == 0
    hg = jnp.ones((1, n), F32)
    if head_gain is not None:
        hg = hg.at[0, :head_norm_cols].set(head_gain)
    return pl.pallas_call(
        functools.partial(_norm_matmul_kernel, head_norm_cols=head_norm_cols),
        grid=(t // PROJ_ROW_TILE,),
        in_specs=[
            pl.BlockSpec((PROJ_ROW_TILE, d), lambda i: (i, 0)),
            pl.BlockSpec((1, d), lambda i: (0, 0)),
            _resident_layer(w, layer),
            pl.BlockSpec((1, n), lambda i: (0, 0)),
        ],
        out_specs=pl.BlockSpec((PROJ_ROW_TILE, n), lambda i: (i, 0)),
        out_shape=jax.ShapeDtypeStruct((t, n), BF16),
        scratch_shapes=[pltpu.VMEM((PROJ_ROW_TILE, d), BF16)],
        compiler_params=pltpu.CompilerParams(
            dimension_semantics=("parallel",),
            vmem_limit_bytes=VMEM_LIMIT),
        name="norm_matmul",
    )(x, gain.reshape(1, d), w, hg)


def _out_ffn_kernel(a_ref, wo_ref, x_ref, g_ref, win_ref, wout_ref, o_ref, h_ref):
    x = x_ref[...] + jnp.dot(a_ref[...], wo_ref[...], preferred_element_type=F32)
    o_ref[...] = x
    h_ref[...] = (x * _rms_scale(x) * g_ref[...]).astype(BF16)
    for c in range(FFN_HIDDEN // FFN_COL):
        lo = c * FFN_COL
        h = h_ref[...]
        gate = jnp.dot(h, win_ref[:, lo:lo + FFN_COL], preferred_element_type=F32)
        up = jnp.dot(h, win_ref[:, FFN_HIDDEN + lo:FFN_HIDDEN + lo + FFN_COL],
                     preferred_element_type=F32)
        act = (gate * jax.nn.sigmoid(gate) * up).astype(BF16)
        o_ref[...] += jnp.dot(act, wout_ref[lo:lo + FFN_COL, :], preferred_element_type=F32)


def _out_ffn(a, w_o, mixer_layer, x, gain, w_in, w_out, layer):
    t, d = x.shape
    k = a.shape[1]
    rows = lambda i: (i, 0)
    whole = lambda i: (0, 0)
    return pl.pallas_call(
        _out_ffn_kernel,
        grid=(t // FFN_ROW_TILE,),
        in_specs=[
            pl.BlockSpec((FFN_ROW_TILE, k), rows),
            _resident_layer(w_o, mixer_layer),
            pl.BlockSpec((FFN_ROW_TILE, d), rows),
            pl.BlockSpec((1, d), whole),
            _resident_layer(w_in, layer),
            _resident_layer(w_out, layer),
        ],
        out_specs=pl.BlockSpec((FFN_ROW_TILE, d), rows),
        out_shape=jax.ShapeDtypeStruct((t, d), F32),
        scratch_shapes=[pltpu.VMEM((FFN_ROW_TILE, d), BF16)],
        compiler_params=pltpu.CompilerParams(
            dimension_semantics=("parallel",),
            vmem_limit_bytes=VMEM_LIMIT),
        name="out_ffn",
    )(a, w_o, x, gain.reshape(1, d), w_in, w_out)


RET_K_COL0 = RET_HEADS * RET_QK_DIM
RET_V_COL0 = 2 * RET_HEADS * RET_QK_DIM
RET_G_COL0 = RET_V_COL0 + RET_HEADS * RET_V_DIM
RET_QDEC_COL0 = RET_G_COL0 + RET_HEADS * RET_V_DIM
RET_KDEC_COL0 = RET_QDEC_COL0 + RET_HEADS * RET_QK_DIM
RET_PROJ_WIDTH = RET_KDEC_COL0 + RET_HEADS * RET_QK_DIM


def _retention_tables(seq):
    h = jnp.arange(RET_HEADS, dtype=F32)
    log_gamma = jnp.log(1.0 - 2.0 ** (-5.0 - h))[:, None, None]
    p = jnp.arange(RET_BLOCK)
    pf = p.astype(F32)
    diff = pf[:, None] - pf[None, :]
    same = (p[:, None] // CHUNK) == (p[None, :] // CHUNK)
    earlier = (p[:, None] // CHUNK) > (p[None, :] // CHUNK)
    k_scale = RET_QK_DIM ** -0.5
    mask = k_scale * jnp.where(same[None], jnp.exp(log_gamma * jnp.abs(diff)[None]),
                               jnp.where(earlier[None], jnp.exp(log_gamma * diff[None]), 0.0))
    tile_pos = (jnp.arange(PROJ_ROW_TILE) % RET_BLOCK).astype(F32)[None, :, None]
    q_decay = jnp.exp(log_gamma * (tile_pos + 1.0))
    k_decay = k_scale * jnp.exp(log_gamma * (RET_BLOCK - 1.0 - tile_pos))
    q_decay = jnp.broadcast_to(q_decay, (RET_HEADS, PROJ_ROW_TILE, LANES))
    k_decay = jnp.broadcast_to(k_decay, (RET_HEADS, PROJ_ROW_TILE, LANES))
    block_decay = jnp.exp(log_gamma[:, 0, 0] * RET_BLOCK)

    half = RET_QK_DIM // 2
    inv_freq = 1.0 / (ROPE_THETA ** (jnp.arange(0, half, dtype=F32) / half))
    ang = jnp.arange(seq).astype(F32)[:, None] * inv_freq[None, :]
    return mask, q_decay, k_decay, block_decay, jnp.cos(ang), jnp.sin(ang)


def _ret_layer_kernel(bd_ref, x_ref, g_ref, w_ref, cos_ref, sin_ref, qd_ref, kd_ref,
                      mask_ref, gain_ref, o_ref, h_ref, proj_ref, state_ref, *, tiles_per_seq):
    s = pl.program_id(0)
    cur = s % 2
    prev = 1 - cur

    @pl.when(s == 0)
    def _():
        proj_ref[1] = jnp.zeros(proj_ref.shape[1:], proj_ref.dtype)

    @pl.when(jnp.logical_or(s == 0, (s - 1) % tiles_per_seq == 0))
    def _():
        state_ref[...] = jnp.zeros_like(state_ref)

    x = x_ref[...]
    h_ref[...] = (x * _rms_scale(x) * g_ref[...]).astype(BF16)
    cos = cos_ref[...]
    sin = sin_ref[...]
    half = RET_QK_DIM // 2

    def project(c):
        lo = c * PROJ_COL
        y = jnp.dot(h_ref[...], w_ref[:, lo:lo + PROJ_COL], preferred_element_type=F32)
        if lo < RET_V_COL0:
            is_q = lo < RET_K_COL0
            dec_ref = qd_ref if is_q else kd_ref
            dec_col0 = RET_QDEC_COL0 if is_q else RET_KDEC_COL0
            rel = lo - (0 if is_q else RET_K_COL0)
            for hh in range(PROJ_COL // RET_QK_DIM):
                head = rel // RET_QK_DIM + hh
                y1 = y[:, hh * RET_QK_DIM:hh * RET_QK_DIM + half]
                y2 = y[:, hh * RET_QK_DIM + half:(hh + 1) * RET_QK_DIM]
                decay = dec_ref[head]
                for part, r in enumerate((y1 * cos - y2 * sin, y1 * sin + y2 * cos)):
                    off = hh * RET_QK_DIM + part * half
                    proj_ref[cur, :, lo + off:lo + off + half] = r.astype(BF16)
                    proj_ref[cur, :, dec_col0 + rel + off:dec_col0 + rel + off + half] = (
                        r * decay).astype(BF16)
        elif lo < RET_G_COL0:
            proj_ref[cur, :, lo:lo + PROJ_COL] = y.astype(BF16)
        else:
            half_g = 0.5 * y
            proj_ref[cur, :, lo:lo + PROJ_COL] = (half_g + half_g * jnp.tanh(half_g)).astype(BF16)

    heads = range(RET_HEADS)
    qk = [slice(h * RET_QK_DIM, (h + 1) * RET_QK_DIM) for h in heads]
    vg = [slice(h * RET_V_DIM, (h + 1) * RET_V_DIM) for h in heads]

    def core_stages(rows):
        def cols(col0, sl):
            return proj_ref[prev, rows, col0 + sl.start:col0 + sl.stop]

        box = {}

        def scores():
            box["scores"] = [lax.dot_general(cols(0, qk[h]), cols(RET_K_COL0, qk[h]), NT_DIMS,
                                             preferred_element_type=F32) for h in heads]

        def outputs():
            box["outs"] = [jnp.dot(
                jnp.concatenate([(box["scores"][h] * mask_ref[h]).astype(BF16),
                                 cols(RET_QDEC_COL0, qk[h])], axis=1),
                jnp.concatenate([cols(RET_V_COL0, vg[h]), state_ref[h].astype(BF16)], axis=0),
                preferred_element_type=F32) for h in heads]

        def update():
            for h in heads:
                upd = lax.dot_general(cols(RET_KDEC_COL0, qk[h]), cols(RET_V_COL0, vg[h]),
                                      TN_DIMS, preferred_element_type=F32)
                state_ref[h] = state_ref[h] * bd_ref[h] + upd

        def finish():
            for h in heads:
                o = box["outs"][h]
                gate = cols(RET_G_COL0, vg[h]).astype(F32)
                o_ref[rows, vg[h]] = (o * _rms_scale(o) * gain_ref[h] * gate).astype(o_ref.dtype)

        return [scores, outputs, update, finish]

    stages = [stage for b in range(PROJ_ROW_TILE // RET_BLOCK)
              for stage in core_stages(slice(b * RET_BLOCK, (b + 1) * RET_BLOCK))]
    chunk = lambda col0, n: [col0 // PROJ_COL + i for i in range(n)]
    per_part = RET_HEADS * RET_QK_DIM // PROJ_COL
    chunks = (chunk(RET_G_COL0, 2 * per_part) + chunk(0, per_part)
              + chunk(RET_K_COL0, per_part) + chunk(RET_V_COL0, 2 * per_part))
    assert sorted(chunks) == list(range(w_ref.shape[1] // PROJ_COL))
    s0, o0, u0, f0, s1, o1, u1, f1 = stages
    plan = [s0, s1, o0, u0, 2, f0, 2, o1, 2, u1, 2, f1, 4]
    done = 0
    for item in plan:
        if callable(item):
            item()
        else:
            for c in chunks[done:done + item]:
                project(c)
            done += item
    assert done == len(chunks)


def _retention(x, norm_gain, w_in, layer, out_gain, batch, seq):
    t, d = x.shape
    tiles = t // PROJ_ROW_TILE
    tiles_per_seq = seq // PROJ_ROW_TILE
    mask, q_decay, k_decay, block_decay, cos, sin = _retention_tables(seq)
    this = lambda s: jnp.minimum(s, tiles - 1)
    last = lambda s: jnp.maximum(s - 1, 0)
    whole2 = lambda s: (0, 0)
    whole3 = lambda s: (0, 0, 0)
    pos = lambda s: (this(s) % tiles_per_seq, 0)
    once = pl.Buffered(1)
    return pl.pallas_call(
        functools.partial(_ret_layer_kernel, tiles_per_seq=tiles_per_seq),
        grid=(tiles + 1,),
        in_specs=[
            pl.BlockSpec(memory_space=pltpu.SMEM),
            pl.BlockSpec((PROJ_ROW_TILE, d), lambda s: (this(s), 0)),
            pl.BlockSpec((1, d), whole2),
            _resident_layer(w_in, layer),
            pl.BlockSpec((PROJ_ROW_TILE, RET_QK_DIM // 2), pos),
            pl.BlockSpec((PROJ_ROW_TILE, RET_QK_DIM // 2), pos),
            pl.BlockSpec((RET_HEADS, PROJ_ROW_TILE, LANES), whole3, pipeline_mode=once),
            pl.BlockSpec((RET_HEADS, PROJ_ROW_TILE, LANES), whole3, pipeline_mode=once),
            pl.BlockSpec((RET_HEADS, RET_BLOCK, RET_BLOCK), whole3, pipeline_mode=once),
            pl.BlockSpec((RET_HEADS, 1, RET_V_DIM), whole3),
        ],
        out_specs=pl.BlockSpec((PROJ_ROW_TILE, RET_HEADS * RET_V_DIM), lambda s: (last(s), 0)),
        out_shape=jax.ShapeDtypeStruct((t, RET_HEADS * RET_V_DIM), BF16),
        scratch_shapes=[
            pltpu.VMEM((PROJ_ROW_TILE, d), BF16),
            pltpu.VMEM((2, PROJ_ROW_TILE, RET_PROJ_WIDTH), BF16),
            pltpu.VMEM((RET_HEADS, RET_QK_DIM, RET_V_DIM), F32),
        ],
        compiler_params=pltpu.CompilerParams(
            dimension_semantics=("arbitrary",),
            vmem_limit_bytes=VMEM_LIMIT),
        name="retention_layer",
    )(block_decay, x, norm_gain.reshape(1, d), w_in, cos, sin, q_decay, k_decay, mask,
      out_gain.reshape(RET_HEADS, 1, RET_V_DIM))


def _sb_kernel(q_ref, k_ref, v_ref, o_ref, qs_ref, vt_ref, acc_ref, carry_ref):
    seq = q_ref.shape[0]
    nblk = seq // SB_BLOCK
    npair = q_ref.shape[1] // LANES
    per_block = LANES // SB_HEAD_DIM
    heads = range(per_block * npair)

    lane = lax.broadcasted_iota(jnp.int32, (1, LANES), 1)
    for p in range(npair):
        cols = slice(p * LANES, (p + 1) * LANES)
        q = q_ref[:, cols]
        for sub in range(per_block):
            keep = jnp.where(lane // SB_HEAD_DIM == sub, 1.0, 0.0).astype(BF16)
            qs_ref[per_block * p + sub] = q * keep
        for kb in range(nblk):
            rows = slice(kb * SB_BLOCK, (kb + 1) * SB_BLOCK)
            vt_ref[p, kb] = v_ref[rows, cols].T

    row = lax.broadcasted_iota(jnp.int32, (SB_BLOCK, SB_BLOCK), 0)
    col = lax.broadcasted_iota(jnp.int32, (SB_BLOCK, SB_BLOCK), 1)
    causal = row < col
    tri = jnp.where(causal, 1.0, 0.0).astype(BF16)

    def tile_group(q0, kbs, from_diagonal):
        jobs = [(j, h) for j in range(len(kbs)) for h in heads]
        k0s = [pl.multiple_of(kb * SB_BLOCK, SB_BLOCK) for kb in kbs]
        zs = {}
        for j, h in jobs:
            p = h // per_block
            zs[j, h] = lax.dot_general(
                k_ref[pl.ds(k0s[j], SB_BLOCK), p * LANES:(p + 1) * LANES],
                qs_ref[h, pl.ds(q0, SB_BLOCK), :], NT_DIMS, preferred_element_type=F32)
        log_beta, later, totals = {}, {}, {}
        for j, h in jobs:
            z = zs[j, h]
            softplus = jnp.maximum(z, 0.0) + jnp.log(1.0 + jnp.exp2(-jnp.abs(z))) * LOG2E
            log_beta[j, h] = z - softplus
            stay = jnp.where(causal, softplus, 0.0) if from_diagonal and j == 0 else softplus
            later[j, h] = jnp.dot(tri, stay.astype(BF16), preferred_element_type=F32)
            totals[j, h] = later[j, h][0:1, :] + stay[0:1, :]
        carries, outs = {}, {}
        for j, h in jobs:
            p, sub = divmod(h, per_block)
            if from_diagonal and j == 0:
                w = jnp.where(causal, jnp.exp2(log_beta[j, h] - later[j, h]), 0.0)
                carries[h] = totals[j, h]
            else:
                carry = carries[h] if h in carries else carry_ref[h]
                w = jnp.exp2(log_beta[j, h] - later[j, h] - carry)
                carries[h] = carry + totals[j, h]
            v_t = vt_ref[p, kbs[j], sub * SB_HEAD_DIM:(sub + 1) * SB_HEAD_DIM, :]
            pv = jnp.dot(v_t, w.astype(BF16), preferred_element_type=F32)
            outs[h] = pv if h not in outs else outs[h] + pv
        least = None
        for h in heads:
            out_rows = slice(h * SB_HEAD_DIM, (h + 1) * SB_HEAD_DIM)
            if from_diagonal:
                acc_ref[out_rows, :] = outs[h]
            else:
                acc_ref[out_rows, :] += outs[h]
            carry_ref[h] = carries[h]
            least = carries[h] if least is None else jnp.minimum(least, carries[h])
        return jnp.min(least)

    def finish(q0):
        o_ref[pl.ds(q0, SB_BLOCK), :] = acc_ref[...].astype(o_ref.dtype).T

    tile_group(0, [0], True)
    finish(0)

    def q_block(qb, _):
        q0 = pl.multiple_of(qb * SB_BLOCK, SB_BLOCK)
        least = tile_group(q0, [qb, qb - 1], True)

        def more(state):
            kb, least = state
            return jnp.logical_and(kb >= 0, least < SB_DEAD_LOG2)

        def k_block(state):
            kb, _ = state
            return kb - 1, tile_group(q0, [kb], False)

        lax.while_loop(more, k_block, (qb - 2, least))
        finish(q0)
        return 0

    lax.fori_loop(1, nblk, q_block, 0)


def _stick_breaking(proj, batch, seq):
    t = proj.shape[0]
    width = SB_GROUP * SB_HEAD_DIM
    groups = SB_HEADS // SB_GROUP
    return pl.pallas_call(
        _sb_kernel,
        grid=(batch, groups),
        in_specs=[
            pl.BlockSpec((seq, width), lambda b, g: (b, g)),
            pl.BlockSpec((seq, width), lambda b, g: (b, groups + g)),
            pl.BlockSpec((seq, width), lambda b, g: (b, 2 * groups + g)),
        ],
        out_specs=pl.BlockSpec((seq, width), lambda b, g: (b, g)),
        out_shape=jax.ShapeDtypeStruct((t, SB_HEADS * SB_HEAD_DIM), BF16),
        scratch_shapes=[
            pltpu.VMEM((SB_GROUP, seq, LANES), BF16),
            pltpu.VMEM((width // LANES, seq // SB_BLOCK, LANES, SB_BLOCK), BF16),
            pltpu.VMEM((width, SB_BLOCK), F32),
            pltpu.VMEM((SB_GROUP, 1, SB_BLOCK), F32),
        ],
        compiler_params=pltpu.CompilerParams(
            dimension_semantics=("parallel", "parallel"),
            vmem_limit_bytes=VMEM_LIMIT),
        name="stick_breaking",
    )(proj, proj, proj)


def kernel(x, mix_norm, ffn_norm, ret_w_in, ret_out_norm, ret_w_o, sb_w_in,
           sb_q_norm, sb_k_norm, sb_w_o, ffn_w_in, ffn_w_out):
    batch, seq, d = x.shape
    assert d == D_MODEL and seq % RET_BLOCK == 0 and seq % SB_BLOCK == 0
    assert (batch * seq) % PROJ_ROW_TILE == 0 and (batch * seq) % FFN_ROW_TILE == 0
    xt = x.reshape(batch * seq, d)
    ret_w_in, ret_w_o, sb_w_in, sb_w_o, ffn_w_in, ffn_w_out = (
        w.astype(BF16) for w in (ret_w_in, ret_w_o, sb_w_in, sb_w_o, ffn_w_in, ffn_w_out))
    for i in range(DEPTH):
        j = i // N_MIXERS
        if i % N_MIXERS == 0:
            mixed = _retention(xt, mix_norm[i], ret_w_in, j, ret_out_norm[j], batch, seq)
            w_o = ret_w_o
        else:
            head_gain = jnp.concatenate([
                jnp.tile(sb_q_norm[j] * (LOG2E * SB_HEAD_DIM ** -0.5), SB_HEADS),
                jnp.tile(sb_k_norm[j], SB_HEADS)])
            proj = _norm_matmul(xt, mix_norm[i], sb_w_in, j, head_gain)
            mixed = _stick_breaking(proj, batch, seq)
            w_o = sb_w_o
        xt = _out_ffn(mixed, w_o, j, xt, ffn_norm[i], ffn_w_in, ffn_w_out, i)
    return xt.reshape(batch, seq, d)
```

```python
import functools
import math

import jax
import jax.numpy as jnp
from jax import lax
from jax.experimental import pallas as pl
from jax.experimental.pallas import tpu as pltpu

D_MODEL = 1024
DEPTH = 4
N_MIXERS = 2
CHUNK = 64
RET_HEADS = 4
RET_QK_DIM = 256
RET_V_DIM = 512
SB_HEADS = 16
SB_HEAD_DIM = 64
FFN_HIDDEN = 2816
ROPE_THETA = 10000.0
EPS = 1e-6

LANES = 128
MXU_DIM = 256
VMEM_LIMIT = 56 * 1024 * 1024

RET_BLOCK = MXU_DIM
SB_BLOCK = MXU_DIM
SB_GROUP = 8
SB_DEAD_LOG2 = 160.0
PROJ_ROW_TILE = 512
PROJ_COL = 512
FFN_ROW_TILE = 1024
FFN_COL = MXU_DIM

F32 = jnp.float32
BF16 = jnp.bfloat16
NT_DIMS = (((1,), (1,)), ((), ()))
TN_DIMS = (((0,), (0,)), ((), ()))
LOG2E = math.log2(math.e)


def _rms_scale(x):
    return lax.rsqrt(jnp.mean(x * x, axis=-1, keepdims=True) + EPS)


def _resident_layer(w, layer):
    return pl.BlockSpec((None,) + w.shape[1:], lambda *_: (layer, 0, 0),
                        pipeline_mode=pl.Buffered(1))


def _head_rms_norm(y, gain):
    first = lax.broadcasted_iota(jnp.int32, y.shape, 1) < SB_HEAD_DIM
    y2 = y * y
    s0 = jnp.sum(jnp.where(first, y2, 0.0), axis=-1, keepdims=True)
    s1 = jnp.sum(jnp.where(first, 0.0, y2), axis=-1, keepdims=True)
    ms = jnp.where(first, s0, s1) * (1.0 / SB_HEAD_DIM)
    return y * lax.rsqrt(ms + EPS) * gain


def _norm_matmul_kernel(x_ref, g_ref, w_ref, hg_ref, o_ref, h_ref, *, head_norm_cols):
    x = x_ref[...]
    h_ref[...] = (x * _rms_scale(x) * g_ref[...]).astype(BF16)
    for c in range(w_ref.shape[1] // PROJ_COL):
        lo = c * PROJ_COL
        y = jnp.dot(h_ref[...], w_ref[:, lo:lo + PROJ_COL], preferred_element_type=F32)
        if lo < head_norm_cols:
            for b in range(PROJ_COL // LANES):
                cols = slice(lo + b * LANES, lo + (b + 1) * LANES)
                o_ref[:, cols] = _head_rms_norm(
                    y[:, b * LANES:(b + 1) * LANES], hg_ref[:, cols]).astype(o_ref.dtype)
        else:
            o_ref[:, lo:lo + PROJ_COL] = y.astype(o_ref.dtype)


def _norm_matmul(x, gain, w, layer, head_gain=None):
    t, d = x.shape
    n = w.shape[2]
    head_norm_cols = 0 if head_gain is None else head_gain.shape[0]
    assert head_norm_cols % PROJ_COL == 0
    hg = jnp.ones((1, n), F32)
    if head_gain is not None:
        hg = hg.at[0, :head_norm_cols].set(head_gain)
    return pl.pallas_call(
        functools.partial(_norm_matmul_kernel, head_norm_cols=head_norm_cols),
        grid=(t // PROJ_ROW_TILE,),
        in_specs=[
            pl.BlockSpec((PROJ_ROW_TILE, d), lambda i: (i, 0)),
            pl.BlockSpec((1, d), lambda i: (0, 0)),
            _resident_layer(w, layer),
            pl.BlockSpec((1, n), lambda i: (0, 0)),
        ],
        out_specs=pl.BlockSpec((PROJ_ROW_TILE, n), lambda i: (i, 0)),
        out_shape=jax.ShapeDtypeStruct((t, n), BF16),
        scratch_shapes=[pltpu.VMEM((PROJ_ROW_TILE, d), BF16)],
        compiler_params=pltpu.CompilerParams(
            dimension_semantics=("parallel",),
            vmem_limit_bytes=VMEM_LIMIT),
        name="norm_matmul",
    )(x, gain.reshape(1, d), w, hg)


def _out_ffn_kernel(a_ref, wo_ref, x_ref, g_ref, win_ref, wout_ref, o_ref, h_ref):
    x = x_ref[...] + jnp.dot(a_ref[...], wo_ref[...], preferred_element_type=F32)
    o_ref[...] = x
    h_ref[...] = (x * _rms_scale(x) * g_ref[...]).astype(BF16)
    for c in range(FFN_HIDDEN // FFN_COL):
        lo = c * FFN_COL
        h = h_ref[...]
        gate = jnp.dot(h, win_ref[:, lo:lo + FFN_COL], preferred_element_type=F32)
        up = jnp.dot(h, win_ref[:, FFN_HIDDEN + lo:FFN_HIDDEN + lo + FFN_COL],
                     preferred_element_type=F32)
        act = (gate * jax.nn.sigmoid(gate) * up).astype(BF16)
        o_ref[...] += jnp.dot(act, wout_ref[lo:lo + FFN_COL, :], preferred_element_type=F32)


def _out_ffn(a, w_o, mixer_layer, x, gain, w_in, w_out, layer):
    t, d = x.shape
    k = a.shape[1]
    rows = lambda i: (i, 0)
    whole = lambda i: (0, 0)
    return pl.pallas_call(
        _out_ffn_kernel,
        grid=(t // FFN_ROW_TILE,),
        in_specs=[
            pl.BlockSpec((FFN_ROW_TILE, k), rows),
            _resident_layer(w_o, mixer_layer),
            pl.BlockSpec((FFN_ROW_TILE, d), rows),
            pl.BlockSpec((1, d), whole),
            _resident_layer(w_in, layer),
            _resident_layer(w_out, layer),
        ],
        out_specs=pl.BlockSpec((FFN_ROW_TILE, d), rows),
        out_shape=jax.ShapeDtypeStruct((t, d), F32),
        scratch_shapes=[pltpu.VMEM((FFN_ROW_TILE, d), BF16)],
        compiler_params=pltpu.CompilerParams(
            dimension_semantics=("parallel",),
            vmem_limit_bytes=VMEM_LIMIT),
        name="out_ffn",
    )(a, w_o, x, gain.reshape(1, d), w_in, w_out)


RET_K_COL0 = RET_HEADS * RET_QK_DIM
RET_V_COL0 = 2 * RET_HEADS * RET_QK_DIM
RET_G_COL0 = RET_V_COL0 + RET_HEADS * RET_V_DIM
RET_QDEC_COL0 = RET_G_COL0 + RET_HEADS * RET_V_DIM
RET_KDEC_COL0 = RET_QDEC_COL0 + RET_HEADS * RET_QK_DIM
RET_PROJ_WIDTH = RET_KDEC_COL0 + RET_HEADS * RET_QK_DIM


def _retention_tables(seq):
    h = jnp.arange(RET_HEADS, dtype=F32)
    log_gamma = jnp.log(1.0 - 2.0 ** (-5.0 - h))[:, None, None]
    p = jnp.arange(RET_BLOCK)
    pf = p.astype(F32)
    diff = pf[:, None] - pf[None, :]
    same = (p[:, None] // CHUNK) == (p[None, :] // CHUNK)
    earlier = (p[:, None] // CHUNK) > (p[None, :] // CHUNK)
    k_scale = RET_QK_DIM ** -0.5
    mask = k_scale * jnp.where(same[None], jnp.exp(log_gamma * jnp.abs(diff)[None]),
                               jnp.where(earlier[None], jnp.exp(log_gamma * diff[None]), 0.0))
    tile_pos = (jnp.arange(PROJ_ROW_TILE) % RET_BLOCK).astype(F32)[None, :, None]
    q_decay = jnp.exp(log_gamma * (tile_pos + 1.0))
    k_decay = k_scale * jnp.exp(log_gamma * (RET_BLOCK - 1.0 - tile_pos))
    q_decay = jnp.broadcast_to(q_decay, (RET_HEADS, PROJ_ROW_TILE, LANES))
    k_decay = jnp.broadcast_to(k_decay, (RET_HEADS, PROJ_ROW_TILE, LANES))
    block_decay = jnp.exp(log_gamma[:, 0, 0] * RET_BLOCK)

    half = RET_QK_DIM // 2
    inv_freq = 1.0 / (ROPE_THETA ** (jnp.arange(0, half, dtype=F32) / half))
    ang = jnp.arange(seq).astype(F32)[:, None] * inv_freq[None, :]
    return mask, q_decay, k_decay, block_decay, jnp.cos(ang), jnp.sin(ang)


def _ret_layer_kernel(bd_ref, x_ref, g_ref, w_ref, cos_ref, sin_ref, qd_ref, kd_ref,
                      mask_ref, gain_ref, o_ref, h_ref, proj_ref, state_ref, *, tiles_per_seq):
    s = pl.program_id(0)
    cur = s % 2
    prev = 1 - cur

    @pl.when(s == 0)
    def _():
        proj_ref[1] = jnp.zeros(proj_ref.shape[1:], proj_ref.dtype)

    @pl.when(jnp.logical_or(s == 0, (s - 1) % tiles_per_seq == 0))
    def _():
        state_ref[...] = jnp.zeros_like(state_ref)

    x = x_ref[...]
    h_ref[...] = (x * _rms_scale(x) * g_ref[...]).astype(BF16)
    cos = cos_ref[...]
    sin = sin_ref[...]
    half = RET_QK_DIM // 2

    def project(c):
        lo = c * PROJ_COL
        y = jnp.dot(h_ref[...], w_ref[:, lo:lo + PROJ_COL], preferred_element_type=F32)
        if lo < RET_V_COL0:
            is_q = lo < RET_K_COL0
            dec_ref = qd_ref if is_q else kd_ref
            dec_col0 = RET_QDEC_COL0 if is_q else RET_KDEC_COL0
            rel = lo - (0 if is_q else RET_K_COL0)
            for hh in range(PROJ_COL // RET_QK_DIM):
                head = rel // RET_QK_DIM + hh
                y1 = y[:, hh * RET_QK_DIM:hh * RET_QK_DIM + half]
                y2 = y[:, hh * RET_QK_DIM + half:(hh + 1) * RET_QK_DIM]
                decay = dec_ref[head]
                for part, r in enumerate((y1 * cos - y2 * sin, y1 * sin + y2 * cos)):
                    off = hh * RET_QK_DIM + part * half
                    proj_ref[cur, :, lo + off:lo + off + half] = r.astype(BF16)
                    proj_ref[cur, :, dec_col0 + rel + off:dec_col0 + rel + off + half] = (
                        r * decay).astype(BF16)
        elif lo < RET_G_COL0:
            proj_ref[cur, :, lo:lo + PROJ_COL] = y.astype(BF16)
        else:
            half_g = 0.5 * y
            proj_ref[cur, :, lo:lo + PROJ_COL] = (half_g + half_g * jnp.tanh(half_g)).astype(BF16)

    heads = range(RET_HEADS)
    qk = [slice(h * RET_QK_DIM, (h + 1) * RET_QK_DIM) for h in heads]
    vg = [slice(h * RET_V_DIM, (h + 1) * RET_V_DIM) for h in heads]

    def core_stages(rows):
        def cols(col0, sl):
            return proj_ref[prev, rows, col0 + sl.start:col0 + sl.stop]

        box = {}

        def scores():
            box["scores"] = [lax.dot_general(cols(0, qk[h]), cols(RET_K_COL0, qk[h]), NT_DIMS,
                                             preferred_element_type=F32) for h in heads]

        def outputs():
            box["outs"] = [jnp.dot(
                jnp.concatenate([(box["scores"][h] * mask_ref[h]).astype(BF16),
                                 cols(RET_QDEC_COL0, qk[h])], axis=1),
                jnp.concatenate([cols(RET_V_COL0, vg[h]), state_ref[h].astype(BF16)], axis=0),
                preferred_element_type=F32) for h in heads]

        def update():
            for h in heads:
                upd = lax.dot_general(cols(RET_KDEC_COL0, qk[h]), cols(RET_V_COL0, vg[h]),
                                      TN_DIMS, preferred_element_type=F32)
                state_ref[h] = state_ref[h] * bd_ref[h] + upd

        def finish():
            for h in heads:
                o = box["outs"][h]
                gate = cols(RET_G_COL0, vg[h]).astype(F32)
                o_ref[rows, vg[h]] = (o * _rms_scale(o) * gain_ref[h] * gate).astype(o_ref.dtype)

        return [scores, outputs, update, finish]

    stages = [stage for b in range(PROJ_ROW_TILE // RET_BLOCK)
              for stage in core_stages(slice(b * RET_BLOCK, (b + 1) * RET_BLOCK))]
    chunk = lambda col0, n: [col0 // PROJ_COL + i for i in range(n)]
    per_part = RET_HEADS * RET_QK_DIM // PROJ_COL
    chunks = (chunk(RET_G_COL0, 2 * per_part) + chunk(0, per_part)
              + chunk(RET_K_COL0, per_part) + chunk(RET_V_COL0, 2 * per_part))
    assert sorted(chunks) == list(range(w_ref.shape[1] // PROJ_COL))
    s0, o0, u0, f0, s1, o1, u1, f1 = stages
    plan = [s0, s1, o0, u0, 2, f0, 2, o1, 2, u1, 2, f1, 4]
    done = 0
    for item in plan:
        if callable(item):
            item()
        else:
            for c in chunks[done:done + item]:
                project(c)
            done += item
    assert done == len(chunks)


def _retention(x, norm_gain, w_in, layer, out_gain, batch, seq):
    t, d = x.shape
    tiles = t // PROJ_ROW_TILE
    tiles_per_seq = seq // PROJ_ROW_TILE
    mask, q_decay, k_decay, block_decay, cos, sin = _retention_tables(seq)
    this = lambda s: jnp.minimum(s, tiles - 1)
    last = lambda s: jnp.maximum(s - 1, 0)
    whole2 = lambda s: (0, 0)
    whole3 = lambda s: (0, 0, 0)
    pos = lambda s: (this(s) % tiles_per_seq, 0)
    once = pl.Buffered(1)
    return pl.pallas_call(
        functools.partial(_ret_layer_kernel, tiles_per_seq=tiles_per_seq),
        grid=(tiles + 1,),
        in_specs=[
            pl.BlockSpec(memory_space=pltpu.SMEM),
            pl.BlockSpec((PROJ_ROW_TILE, d), lambda s: (this(s), 0)),
            pl.BlockSpec((1, d), whole2),
            _resident_layer(w_in, layer),
            pl.BlockSpec((PROJ_ROW_TILE, RET_QK_DIM // 2), pos),
            pl.BlockSpec((PROJ_ROW_TILE, RET_QK_DIM // 2), pos),
            pl.BlockSpec((RET_HEADS, PROJ_ROW_TILE, LANES), whole3, pipeline_mode=once),
            pl.BlockSpec((RET_HEADS, PROJ_ROW_TILE, LANES), whole3, pipeline_mode=once),
            pl.BlockSpec((RET_HEADS, RET_BLOCK, RET_BLOCK), whole3, pipeline_mode=once),
            pl.BlockSpec((RET_HEADS, 1, RET_V_DIM), whole3),
        ],
        out_specs=pl.BlockSpec((PROJ_ROW_TILE, RET_HEADS * RET_V_DIM), lambda s: (last(s), 0)),
        out_shape=jax.ShapeDtypeStruct((t, RET_HEADS * RET_V_DIM), BF16),
        scratch_shapes=[
            pltpu.VMEM((PROJ_ROW_TILE, d), BF16),
            pltpu.VMEM((2, PROJ_ROW_TILE, RET_PROJ_WIDTH), BF16),
            pltpu.VMEM((RET_HEADS, RET_QK_DIM, RET_V_DIM), F32),
        ],
        compiler_params=pltpu.CompilerParams(
            dimension_semantics=("arbitrary",),
            vmem_limit_bytes=VMEM_LIMIT),
        name="retention_layer",
    )(block_decay, x, norm_gain.reshape(1, d), w_in, cos, sin, q_decay, k_decay, mask,
      out_gain.reshape(RET_HEADS, 1, RET_V_DIM))


def _sb_kernel(q_ref, k_ref, v_ref, o_ref, qs_ref, vt_ref, acc_ref, carry_ref):
    seq = q_ref.shape[0]
    nblk = seq // SB_BLOCK
    npair = q_ref.shape[1] // LANES
    per_block = LANES // SB_HEAD_DIM
    heads = range(per_block * npair)

    lane = lax.broadcasted_iota(jnp.int32, (1, LANES), 1)
    for p in range(npair):
        cols = slice(p * LANES, (p + 1) * LANES)
        q = q_ref[:, cols]
        for sub in range(per_block):
            keep = jnp.where(lane // SB_HEAD_DIM == sub, 1.0, 0.0).astype(BF16)
            qs_ref[per_block * p + sub] = q * keep
        for kb in range(nblk):
            rows = slice(kb * SB_BLOCK, (kb + 1) * SB_BLOCK)
            vt_ref[p, kb] = v_ref[rows, cols].T

    row = lax.broadcasted_iota(jnp.int32, (SB_BLOCK, SB_BLOCK), 0)
    col = lax.broadcasted_iota(jnp.int32, (SB_BLOCK, SB_BLOCK), 1)
    causal = row < col
    tri = jnp.where(causal, 1.0, 0.0).astype(BF16)

    def write_back(qb):
        q0 = pl.multiple_of(qb * SB_BLOCK, SB_BLOCK)
        o_ref[pl.ds(q0, SB_BLOCK), :] = acc_ref[qb % 2].astype(o_ref.dtype).T

    def tile_group(qb, kbs, from_diagonal, flush_previous=False):
        q0 = pl.multiple_of(qb * SB_BLOCK, SB_BLOCK)
        slot = qb % 2
        if flush_previous:
            write_back(qb - 1)
        jobs = [(j, h) for j in range(len(kbs)) for h in heads]
        k0s = [pl.multiple_of(kb * SB_BLOCK, SB_BLOCK) for kb in kbs]
        zs = {}
        for j, h in jobs:
            p = h // per_block
            zs[j, h] = lax.dot_general(
                k_ref[pl.ds(k0s[j], SB_BLOCK), p * LANES:(p + 1) * LANES],
                qs_ref[h, pl.ds(q0, SB_BLOCK), :], NT_DIMS, preferred_element_type=F32)
        log_beta, later, totals = {}, {}, {}
        for j, h in jobs:
            z = zs[j, h]
            softplus = jnp.maximum(z, 0.0) + jnp.log(1.0 + jnp.exp2(-jnp.abs(z))) * LOG2E
            log_beta[j, h] = z - softplus
            stay = jnp.where(causal, softplus, 0.0) if from_diagonal and j == 0 else softplus
            later[j, h] = jnp.dot(tri, stay.astype(BF16), preferred_element_type=F32)
            totals[j, h] = later[j, h][0:1, :] + stay[0:1, :]
        carries, outs = {}, {}
        for j, h in jobs:
            p, sub = divmod(h, per_block)
            if from_diagonal and j == 0:
                w = jnp.where(causal, jnp.exp2(log_beta[j, h] - later[j, h]), 0.0)
                carries[h] = totals[j, h]
            else:
                carry = carries[h] if h in carries else carry_ref[h]
                w = jnp.exp2(log_beta[j, h] - later[j, h] - carry)
                carries[h] = carry + totals[j, h]
            v_t = vt_ref[p, kbs[j], sub * SB_HEAD_DIM:(sub + 1) * SB_HEAD_DIM, :]
            pv = jnp.dot(v_t, w.astype(BF16), preferred_element_type=F32)
            outs[h] = pv if h not in outs else outs[h] + pv
        least = None
        for h in heads:
            out_rows = slice(h * SB_HEAD_DIM, (h + 1) * SB_HEAD_DIM)
            if from_diagonal:
                acc_ref[slot, out_rows, :] = outs[h]
            else:
                acc_ref[slot, out_rows, :] += outs[h]
            carry_ref[h] = carries[h]
            least = carries[h] if least is None else jnp.minimum(least, carries[h])
        return jnp.min(least)

    tile_group(0, [0], True)

    def q_block(qb, _):
        least = tile_group(qb, [qb, qb - 1], True, flush_previous=True)

        def more(state):
            kb, least = state
            return jnp.logical_and(kb >= 0, least < SB_DEAD_LOG2)

        def k_block(state):
            kb, _ = state
            return kb - 1, tile_group(qb, [kb], False)

        lax.while_loop(more, k_block, (qb - 2, least))
        return 0

    lax.fori_loop(1, nblk, q_block, 0)
    write_back(nblk - 1)


def _stick_breaking(proj, batch, seq):
    t = proj.shape[0]
    width = SB_GROUP * SB_HEAD_DIM
    groups = SB_HEADS // SB_GROUP
    return pl.pallas_call(
        _sb_kernel,
        grid=(batch, groups),
        in_specs=[
            pl.BlockSpec((seq, width), lambda b, g: (b, g)),
            pl.BlockSpec((seq, width), lambda b, g: (b, groups + g)),
            pl.BlockSpec((seq, width), lambda b, g: (b, 2 * groups + g)),
        ],
        out_specs=pl.BlockSpec((seq, width), lambda b, g: (b, g)),
        out_shape=jax.ShapeDtypeStruct((t, SB_HEADS * SB_HEAD_DIM), BF16),
        scratch_shapes=[
            pltpu.VMEM((SB_GROUP, seq, LANES), BF16),
            pltpu.VMEM((width // LANES, seq // SB_BLOCK, LANES, SB_BLOCK), BF16),
            pltpu.VMEM((2, width, SB_BLOCK), F32),
            pltpu.VMEM((SB_GROUP, 1, SB_BLOCK), F32),
        ],
        compiler_params=pltpu.CompilerParams(
            dimension_semantics=("parallel", "parallel"),
            vmem_limit_bytes=VMEM_LIMIT),
        name="stick_breaking",
    )(proj, proj, proj)


def kernel(x, mix_norm, ffn_norm, ret_w_in, ret_out_norm, ret_w_o, sb_w_in,
           sb_q_norm, sb_k_norm, sb_w_o, ffn_w_in, ffn_w_out):
    batch, seq, d = x.shape
    assert d == D_MODEL and seq % RET_BLOCK == 0 and seq % SB_BLOCK == 0
    assert (batch * seq) % PROJ_ROW_TILE == 0 and (batch * seq) % FFN_ROW_TILE == 0
    xt = x.reshape(batch * seq, d)
    ret_w_in, ret_w_o, sb_w_in, sb_w_o, ffn_w_in, ffn_w_out = (
        w.astype(BF16) for w in (ret_w_in, ret_w_o, sb_w_in, sb_w_o, ffn_w_in, ffn_w_out))
    for i in range(DEPTH):
        j = i // N_MIXERS
        if i % N_MIXERS == 0:
            mixed = _retention(xt, mix_norm[i], ret_w_in, j, ret_out_norm[j], batch, seq)
            w_o = ret_w_o
        else:
            head_gain = jnp.concatenate([
                jnp.tile(sb_q_norm[j] * (LOG2E * SB_HEAD_DIM ** -0.5), SB_HEADS),
                jnp.tile(sb_k_norm[j], SB_HEADS)])
            proj = _norm_matmul(xt, mix_norm[i], sb_w_in, j, head_gain)
            mixed = _stick_breaking(proj, batch, seq)
            w_o = sb_w_o
        xt = _out_ffn(mixed, w_o, j, xt, ffn_norm[i], ffn_w_in, ffn_w_out, i)
    return xt.reshape(batch, seq, d)
```

```python
import functools
import math

import jax
import jax.numpy as jnp
from jax import lax
from jax.experimental import pallas as pl
from jax.experimental.pallas import tpu as pltpu

D_MODEL = 1024
DEPTH = 4
N_MIXERS = 2
CHUNK = 64
RET_HEADS = 4
RET_QK_DIM = 256
RET_V_DIM = 512
SB_HEADS = 16
SB_HEAD_DIM = 64
FFN_HIDDEN = 2816
ROPE_THETA = 10000.0
EPS = 1e-6

LANES = 128
MXU_DIM = 256
VMEM_LIMIT = 56 * 1024 * 1024

RET_BLOCK = MXU_DIM
SB_BLOCK = MXU_DIM
SB_GROUP = 8
SB_WAVE = 4
SB_DEAD_LOG2 = 160.0
PROJ_ROW_TILE = 512
PROJ_COL = 512
FFN_ROW_TILE = 1024
FFN_COL = MXU_DIM

F32 = jnp.float32
BF16 = jnp.bfloat16
NT_DIMS = (((1,), (1,)), ((), ()))
TN_DIMS = (((0,), (0,)), ((), ()))
LOG2E = math.log2(math.e)


def _rms_scale(x):
    return lax.rsqrt(jnp.mean(x * x, axis=-1, keepdims=True) + EPS)


def _resident_layer(w, layer):
    return pl.BlockSpec((None,) + w.shape[1:], lambda *_: (layer, 0, 0),
                        pipeline_mode=pl.Buffered(1))


def _head_rms_norm(y, gain):
    first = lax.broadcasted_iota(jnp.int32, y.shape, 1) < SB_HEAD_DIM
    y2 = y * y
    s0 = jnp.sum(jnp.where(first, y2, 0.0), axis=-1, keepdims=True)
    s1 = jnp.sum(jnp.where(first, 0.0, y2), axis=-1, keepdims=True)
    ms = jnp.where(first, s0, s1) * (1.0 / SB_HEAD_DIM)
    return y * lax.rsqrt(ms + EPS) * gain


def _norm_matmul_kernel(x_ref, g_ref, w_ref, hg_ref, o_ref, h_ref, *, head_norm_cols):
    x = x_ref[...]
    h_ref[...] = (x * _rms_scale(x) * g_ref[...]).astype(BF16)
    for c in range(w_ref.shape[1] // PROJ_COL):
        lo = c * PROJ_COL
        y = jnp.dot(h_ref[...], w_ref[:, lo:lo + PROJ_COL], preferred_element_type=F32)
        if lo < head_norm_cols:
            for b in range(PROJ_COL // LANES):
                cols = slice(lo + b * LANES, lo + (b + 1) * LANES)
                o_ref[:, cols] = _head_rms_norm(
                    y[:, b * LANES:(b + 1) * LANES], hg_ref[:, cols]).astype(o_ref.dtype)
        else:
            o_ref[:, lo:lo + PROJ_COL] = y.astype(o_ref.dtype)


def _norm_matmul(x, gain, w, layer, head_gain=None):
    t, d = x.shape
    n = w.shape[2]
    head_norm_cols = 0 if head_gain is None else head_gain.shape[0]
    assert head_norm_cols % PROJ_COL == 0
    hg = jnp.ones((1, n), F32)
    if head_gain is not None:
        hg = hg.at[0, :head_norm_cols].set(head_gain)
    return pl.pallas_call(
        functools.partial(_norm_matmul_kernel, head_norm_cols=head_norm_cols),
        grid=(t // PROJ_ROW_TILE,),
        in_specs=[
            pl.BlockSpec((PROJ_ROW_TILE, d), lambda i: (i, 0)),
            pl.BlockSpec((1, d), lambda i: (0, 0)),
            _resident_layer(w, layer),
            pl.BlockSpec((1, n), lambda i: (0, 0)),
        ],
        out_specs=pl.BlockSpec((PROJ_ROW_TILE, n), lambda i: (i, 0)),
        out_shape=jax.ShapeDtypeStruct((t, n), BF16),
        scratch_shapes=[pltpu.VMEM((PROJ_ROW_TILE, d), BF16)],
        compiler_params=pltpu.CompilerParams(
            dimension_semantics=("parallel",),
            vmem_limit_bytes=VMEM_LIMIT),
        name="norm_matmul",
    )(x, gain.reshape(1, d), w, hg)


def _out_ffn_kernel(a_ref, wo_ref, x_ref, g_ref, win_ref, wout_ref, o_ref, h_ref):
    x = x_ref[...] + jnp.dot(a_ref[...], wo_ref[...], preferred_element_type=F32)
    o_ref[...] = x
    h_ref[...] = (x * _rms_scale(x) * g_ref[...]).astype(BF16)
    for c in range(FFN_HIDDEN // FFN_COL):
        lo = c * FFN_COL
        h = h_ref[...]
        gate = jnp.dot(h, win_ref[:, lo:lo + FFN_COL], preferred_element_type=F32)
        up = jnp.dot(h, win_ref[:, FFN_HIDDEN + lo:FFN_HIDDEN + lo + FFN_COL],
                     preferred_element_type=F32)
        act = (gate * jax.nn.sigmoid(gate) * up).astype(BF16)
        o_ref[...] += jnp.dot(act, wout_ref[lo:lo + FFN_COL, :], preferred_element_type=F32)


def _out_ffn(a, w_o, mixer_layer, x, gain, w_in, w_out, layer):
    t, d = x.shape
    k = a.shape[1]
    rows = lambda i: (i, 0)
    whole = lambda i: (0, 0)
    return pl.pallas_call(
        _out_ffn_kernel,
        grid=(t // FFN_ROW_TILE,),
        in_specs=[
            pl.BlockSpec((FFN_ROW_TILE, k), rows),
            _resident_layer(w_o, mixer_layer),
            pl.BlockSpec((FFN_ROW_TILE, d), rows),
            pl.BlockSpec((1, d), whole),
            _resident_layer(w_in, layer),
            _resident_layer(w_out, layer),
        ],
        out_specs=pl.BlockSpec((FFN_ROW_TILE, d), rows),
        out_shape=jax.ShapeDtypeStruct((t, d), F32),
        scratch_shapes=[pltpu.VMEM((FFN_ROW_TILE, d), BF16)],
        compiler_params=pltpu.CompilerParams(
            dimension_semantics=("parallel",),
            vmem_limit_bytes=VMEM_LIMIT),
        name="out_ffn",
    )(a, w_o, x, gain.reshape(1, d), w_in, w_out)


RET_K_COL0 = RET_HEADS * RET_QK_DIM
RET_V_COL0 = 2 * RET_HEADS * RET_QK_DIM
RET_G_COL0 = RET_V_COL0 + RET_HEADS * RET_V_DIM
RET_QDEC_COL0 = RET_G_COL0 + RET_HEADS * RET_V_DIM
RET_KDEC_COL0 = RET_QDEC_COL0 + RET_HEADS * RET_QK_DIM
RET_PROJ_WIDTH = RET_KDEC_COL0 + RET_HEADS * RET_QK_DIM


def _retention_tables(seq):
    h = jnp.arange(RET_HEADS, dtype=F32)
    log_gamma = jnp.log(1.0 - 2.0 ** (-5.0 - h))[:, None, None]
    p = jnp.arange(RET_BLOCK)
    pf = p.astype(F32)
    diff = pf[:, None] - pf[None, :]
    same = (p[:, None] // CHUNK) == (p[None, :] // CHUNK)
    earlier = (p[:, None] // CHUNK) > (p[None, :] // CHUNK)
    k_scale = RET_QK_DIM ** -0.5
    mask = k_scale * jnp.where(same[None], jnp.exp(log_gamma * jnp.abs(diff)[None]),
                               jnp.where(earlier[None], jnp.exp(log_gamma * diff[None]), 0.0))
    tile_pos = (jnp.arange(PROJ_ROW_TILE) % RET_BLOCK).astype(F32)[None, :, None]
    q_decay = jnp.exp(log_gamma * (tile_pos + 1.0))
    k_decay = k_scale * jnp.exp(log_gamma * (RET_BLOCK - 1.0 - tile_pos))
    q_decay = jnp.broadcast_to(q_decay, (RET_HEADS, PROJ_ROW_TILE, LANES))
    k_decay = jnp.broadcast_to(k_decay, (RET_HEADS, PROJ_ROW_TILE, LANES))
    block_decay = jnp.exp(log_gamma[:, 0, 0] * RET_BLOCK)

    half = RET_QK_DIM // 2
    inv_freq = 1.0 / (ROPE_THETA ** (jnp.arange(0, half, dtype=F32) / half))
    ang = jnp.arange(seq).astype(F32)[:, None] * inv_freq[None, :]
    return mask, q_decay, k_decay, block_decay, jnp.cos(ang), jnp.sin(ang)


def _ret_layer_kernel(bd_ref, x_ref, g_ref, w_ref, cos_ref, sin_ref, qd_ref, kd_ref,
                      mask_ref, gain_ref, o_ref, h_ref, proj_ref, state_ref, *, tiles_per_seq):
    s = pl.program_id(0)
    cur = s % 2
    prev = 1 - cur

    @pl.when(s == 0)
    def _():
        proj_ref[1] = jnp.zeros(proj_ref.shape[1:], proj_ref.dtype)

    @pl.when(jnp.logical_or(s == 0, (s - 1) % tiles_per_seq == 0))
    def _():
        state_ref[...] = jnp.zeros_like(state_ref)

    x = x_ref[...]
    h_ref[...] = (x * _rms_scale(x) * g_ref[...]).astype(BF16)
    cos = cos_ref[...]
    sin = sin_ref[...]
    half = RET_QK_DIM // 2

    def project(c):
        lo = c * PROJ_COL
        y = jnp.dot(h_ref[...], w_ref[:, lo:lo + PROJ_COL], preferred_element_type=F32)
        if lo < RET_V_COL0:
            is_q = lo < RET_K_COL0
            dec_ref = qd_ref if is_q else kd_ref
            dec_col0 = RET_QDEC_COL0 if is_q else RET_KDEC_COL0
            rel = lo - (0 if is_q else RET_K_COL0)
            for hh in range(PROJ_COL // RET_QK_DIM):
                head = rel // RET_QK_DIM + hh
                y1 = y[:, hh * RET_QK_DIM:hh * RET_QK_DIM + half]
                y2 = y[:, hh * RET_QK_DIM + half:(hh + 1) * RET_QK_DIM]
                decay = dec_ref[head]
                for part, r in enumerate((y1 * cos - y2 * sin, y1 * sin + y2 * cos)):
                    off = hh * RET_QK_DIM + part * half
                    proj_ref[cur, :, lo + off:lo + off + half] = r.astype(BF16)
                    proj_ref[cur, :, dec_col0 + rel + off:dec_col0 + rel + off + half] = (
                        r * decay).astype(BF16)
        elif lo < RET_G_COL0:
            proj_ref[cur, :, lo:lo + PROJ_COL] = y.astype(BF16)
        else:
            half_g = 0.5 * y
            proj_ref[cur, :, lo:lo + PROJ_COL] = (half_g + half_g * jnp.tanh(half_g)).astype(BF16)

    heads = range(RET_HEADS)
    qk = [slice(h * RET_QK_DIM, (h + 1) * RET_QK_DIM) for h in heads]
    vg = [slice(h * RET_V_DIM, (h + 1) * RET_V_DIM) for h in heads]

    def core_stages(rows):
        def cols(col0, sl):
            return proj_ref[prev, rows, col0 + sl.start:col0 + sl.stop]

        box = {}

        def scores():
            box["scores"] = [lax.dot_general(cols(0, qk[h]), cols(RET_K_COL0, qk[h]), NT_DIMS,
                                             preferred_element_type=F32) for h in heads]

        def outputs():
            box["outs"] = [jnp.dot(
                jnp.concatenate([(box["scores"][h] * mask_ref[h]).astype(BF16),
                                 cols(RET_QDEC_COL0, qk[h])], axis=1),
                jnp.concatenate([cols(RET_V_COL0, vg[h]), state_ref[h].astype(BF16)], axis=0),
                preferred_element_type=F32) for h in heads]

        def update():
            for h in heads:
                upd = lax.dot_general(cols(RET_KDEC_COL0, qk[h]), cols(RET_V_COL0, vg[h]),
                                      TN_DIMS, preferred_element_type=F32)
                state_ref[h] = state_ref[h] * bd_ref[h] + upd

        def finish():
            for h in heads:
                o = box["outs"][h]
                gate = cols(RET_G_COL0, vg[h]).astype(F32)
                o_ref[rows, vg[h]] = (o * _rms_scale(o) * gain_ref[h] * gate).astype(o_ref.dtype)

        return [scores, outputs, update, finish]

    stages = [stage for b in range(PROJ_ROW_TILE // RET_BLOCK)
              for stage in core_stages(slice(b * RET_BLOCK, (b + 1) * RET_BLOCK))]
    chunk = lambda col0, n: [col0 // PROJ_COL + i for i in range(n)]
    per_part = RET_HEADS * RET_QK_DIM // PROJ_COL
    chunks = (chunk(RET_G_COL0, 2 * per_part) + chunk(0, per_part)
              + chunk(RET_K_COL0, per_part) + chunk(RET_V_COL0, 2 * per_part))
    assert sorted(chunks) == list(range(w_ref.shape[1] // PROJ_COL))
    s0, o0, u0, f0, s1, o1, u1, f1 = stages
    plan = [s0, s1, o0, u0, 2, f0, 2, o1, 2, u1, 2, f1, 4]
    done = 0
    for item in plan:
        if callable(item):
            item()
        else:
            for c in chunks[done:done + item]:
                project(c)
            done += item
    assert done == len(chunks)


def _retention(x, norm_gain, w_in, layer, out_gain, batch, seq):
    t, d = x.shape
    tiles = t // PROJ_ROW_TILE
    tiles_per_seq = seq // PROJ_ROW_TILE
    mask, q_decay, k_decay, block_decay, cos, sin = _retention_tables(seq)
    this = lambda s: jnp.minimum(s, tiles - 1)
    last = lambda s: jnp.maximum(s - 1, 0)
    whole2 = lambda s: (0, 0)
    whole3 = lambda s: (0, 0, 0)
    pos = lambda s: (this(s) % tiles_per_seq, 0)
    once = pl.Buffered(1)
    return pl.pallas_call(
        functools.partial(_ret_layer_kernel, tiles_per_seq=tiles_per_seq),
        grid=(tiles + 1,),
        in_specs=[
            pl.BlockSpec(memory_space=pltpu.SMEM),
            pl.BlockSpec((PROJ_ROW_TILE, d), lambda s: (this(s), 0)),
            pl.BlockSpec((1, d), whole2),
            _resident_layer(w_in, layer),
            pl.BlockSpec((PROJ_ROW_TILE, RET_QK_DIM // 2), pos),
            pl.BlockSpec((PROJ_ROW_TILE, RET_QK_DIM // 2), pos),
            pl.BlockSpec((RET_HEADS, PROJ_ROW_TILE, LANES), whole3, pipeline_mode=once),
            pl.BlockSpec((RET_HEADS, PROJ_ROW_TILE, LANES), whole3, pipeline_mode=once),
            pl.BlockSpec((RET_HEADS, RET_BLOCK, RET_BLOCK), whole3, pipeline_mode=once),
            pl.BlockSpec((RET_HEADS, 1, RET_V_DIM), whole3),
        ],
        out_specs=pl.BlockSpec((PROJ_ROW_TILE, RET_HEADS * RET_V_DIM), lambda s: (last(s), 0)),
        out_shape=jax.ShapeDtypeStruct((t, RET_HEADS * RET_V_DIM), BF16),
        scratch_shapes=[
            pltpu.VMEM((PROJ_ROW_TILE, d), BF16),
            pltpu.VMEM((2, PROJ_ROW_TILE, RET_PROJ_WIDTH), BF16),
            pltpu.VMEM((RET_HEADS, RET_QK_DIM, RET_V_DIM), F32),
        ],
        compiler_params=pltpu.CompilerParams(
            dimension_semantics=("arbitrary",),
            vmem_limit_bytes=VMEM_LIMIT),
        name="retention_layer",
    )(block_decay, x, norm_gain.reshape(1, d), w_in, cos, sin, q_decay, k_decay, mask,
      out_gain.reshape(RET_HEADS, 1, RET_V_DIM))


def _sb_kernel(q_ref, k_ref, v_ref, o_ref, qs_ref, vt_ref, acc_ref, carry_ref):
    seq = q_ref.shape[0]
    nblk = seq // SB_BLOCK
    npair = q_ref.shape[1] // LANES
    per_block = LANES // SB_HEAD_DIM
    heads = range(per_block * npair)

    lane = lax.broadcasted_iota(jnp.int32, (1, LANES), 1)
    for p in range(npair):
        cols = slice(p * LANES, (p + 1) * LANES)
        q = q_ref[:, cols]
        for sub in range(per_block):
            keep = jnp.where(lane // SB_HEAD_DIM == sub, 1.0, 0.0).astype(BF16)
            qs_ref[per_block * p + sub] = q * keep
        for kb in range(nblk):
            rows = slice(kb * SB_BLOCK, (kb + 1) * SB_BLOCK)
            vt_ref[p, kb] = v_ref[rows, cols].T

    row = lax.broadcasted_iota(jnp.int32, (SB_BLOCK, SB_BLOCK), 0)
    col = lax.broadcasted_iota(jnp.int32, (SB_BLOCK, SB_BLOCK), 1)
    causal = row < col
    tri = jnp.where(causal, 1.0, 0.0).astype(BF16)

    def write_back(qb):
        q0 = pl.multiple_of(qb * SB_BLOCK, SB_BLOCK)
        o_ref[pl.ds(q0, SB_BLOCK), :] = acc_ref[qb % 2].astype(o_ref.dtype).T

    def tile_group(qb, kbs, from_diagonal, flush_previous=False):
        q0 = pl.multiple_of(qb * SB_BLOCK, SB_BLOCK)
        slot = qb % 2
        if flush_previous:
            write_back(qb - 1)
        k0s = [pl.multiple_of(kb * SB_BLOCK, SB_BLOCK) for kb in kbs]
        zs, log_beta, later, totals, carries, outs = {}, {}, {}, {}, {}, {}

        def scores(jobs):
            for j, h in jobs:
                p = h // per_block
                zs[j, h] = lax.dot_general(
                    k_ref[pl.ds(k0s[j], SB_BLOCK), p * LANES:(p + 1) * LANES],
                    qs_ref[h, pl.ds(q0, SB_BLOCK), :], NT_DIMS, preferred_element_type=F32)

        def cumulate(jobs):
            for j, h in jobs:
                z = zs.pop((j, h))
                softplus = jnp.maximum(z, 0.0) + jnp.log(1.0 + jnp.exp2(-jnp.abs(z))) * LOG2E
                log_beta[j, h] = z - softplus
                stay = jnp.where(causal, softplus, 0.0) if from_diagonal and j == 0 else softplus
                later[j, h] = jnp.dot(tri, stay.astype(BF16), preferred_element_type=F32)
                totals[j, h] = later[j, h][0:1, :] + stay[0:1, :]

        def weigh(jobs):
            for j, h in jobs:
                p, sub = divmod(h, per_block)
                if from_diagonal and j == 0:
                    w = jnp.where(causal, jnp.exp2(log_beta.pop((j, h)) - later.pop((j, h))), 0.0)
                    carries[h] = totals[j, h]
                else:
                    carry = carries[h] if h in carries else carry_ref[h]
                    w = jnp.exp2(log_beta.pop((j, h)) - later.pop((j, h)) - carry)
                    carries[h] = carry + totals[j, h]
                v_t = vt_ref[p, kbs[j], sub * SB_HEAD_DIM:(sub + 1) * SB_HEAD_DIM, :]
                pv = jnp.dot(v_t, w.astype(BF16), preferred_element_type=F32)
                outs[h] = pv if h not in outs else outs[h] + pv

        waves = [[(j, h) for h in heads[i:i + SB_WAVE]]
                 for j in range(len(kbs)) for i in range(0, len(heads), SB_WAVE)]
        stages = (scores, cumulate, weigh)
        for step in range(len(waves) + len(stages) - 1):
            for depth, stage in enumerate(stages):
                if 0 <= step - depth < len(waves):
                    stage(waves[step - depth])
        least = None
        for h in heads:
            out_rows = slice(h * SB_HEAD_DIM, (h + 1) * SB_HEAD_DIM)
            if from_diagonal:
                acc_ref[slot, out_rows, :] = outs[h]
            else:
                acc_ref[slot, out_rows, :] += outs[h]
            carry_ref[h] = carries[h]
            least = carries[h] if least is None else jnp.minimum(least, carries[h])
        return jnp.min(least)

    tile_group(0, [0], True)

    def q_block(qb, _):
        least = tile_group(qb, [qb, qb - 1], True, flush_previous=True)

        def more(state):
            kb, least = state
            return jnp.logical_and(kb >= 0, least < SB_DEAD_LOG2)

        def k_block(state):
            kb, _ = state
            return kb - 1, tile_group(qb, [kb], False)

        lax.while_loop(more, k_block, (qb - 2, least))
        return 0

    lax.fori_loop(1, nblk, q_block, 0)
    write_back(nblk - 1)


def _stick_breaking(proj, batch, seq):
    t = proj.shape[0]
    width = SB_GROUP * SB_HEAD_DIM
    groups = SB_HEADS // SB_GROUP
    return pl.pallas_call(
        _sb_kernel,
        grid=(batch, groups),
        in_specs=[
            pl.BlockSpec((seq, width), lambda b, g: (b, g)),
            pl.BlockSpec((seq, width), lambda b, g: (b, groups + g)),
            pl.BlockSpec((seq, width), lambda b, g: (b, 2 * groups + g)),
        ],
        out_specs=pl.BlockSpec((seq, width), lambda b, g: (b, g)),
        out_shape=jax.ShapeDtypeStruct((t, SB_HEADS * SB_HEAD_DIM), BF16),
        scratch_shapes=[
            pltpu.VMEM((SB_GROUP, seq, LANES), BF16),
            pltpu.VMEM((width // LANES, seq // SB_BLOCK, LANES, SB_BLOCK), BF16),
            pltpu.VMEM((2, width, SB_BLOCK), F32),
            pltpu.VMEM((SB_GROUP, 1, SB_BLOCK), F32),
        ],
        compiler_params=pltpu.CompilerParams(
            dimension_semantics=("parallel", "parallel"),
            vmem_limit_bytes=VMEM_LIMIT),
        name="stick_breaking",
    )(proj, proj, proj)


def kernel(x, mix_norm, ffn_norm, ret_w_in, ret_out_norm, ret_w_o, sb_w_in,
           sb_q_norm, sb_k_norm, sb_w_o, ffn_w_in, ffn_w_out):
    batch, seq, d = x.shape
    assert d == D_MODEL and seq % RET_BLOCK == 0 and seq % SB_BLOCK == 0
    assert (batch * seq) % PROJ_ROW_TILE == 0 and (batch * seq) % FFN_ROW_TILE == 0
    xt = x.reshape(batch * seq, d)
    ret_w_in, ret_w_o, sb_w_in, sb_w_o, ffn_w_in, ffn_w_out = (
        w.astype(BF16) for w in (ret_w_in, ret_w_o, sb_w_in, sb_w_o, ffn_w_in, ffn_w_out))
    for i in range(DEPTH):
        j = i // N_MIXERS
        if i % N_MIXERS == 0:
            mixed = _retention(xt, mix_norm[i], ret_w_in, j, ret_out_norm[j], batch, seq)
            w_o = ret_w_o
        else:
            head_gain = jnp.concatenate([
                jnp.tile(sb_q_norm[j] * (LOG2E * SB_HEAD_DIM ** -0.5), SB_HEADS),
                jnp.tile(sb_k_norm[j], SB_HEADS)])
            proj = _norm_matmul(xt, mix_norm[i], sb_w_in, j, head_gain)
            mixed = _stick_breaking(proj, batch, seq)
            w_o = sb_w_o
        xt = _out_ffn(mixed, w_o, j, xt, ffn_norm[i], ffn_w_in, ffn_w_out, i)
    return xt.reshape(batch, seq, d)
```
